```python
import jax, jax.numpy as jnp
from jax import lax
import numpy as np

D_MODEL = 2048
BATCH = 32
SEQ = 256
DEPTH = 2
DEC_BATCH = 4
DEC_SEQ = 2048
PAST_LEN = 512

GRID_W = 64
D_FOURIER = D_MODEL // 2
D_LRU = D_MODEL // 2
D_MIX = D_FOURIER + D_LRU
D_IN = D_FOURIER + 2 * D_LRU
N_FOURIER_HEADS = 4
DH_FOURIER = D_FOURIER // N_FOURIER_HEADS
N_LRU_HEADS = 4
DH_LRU = D_LRU // N_LRU_HEADS
CONV_W = 4
CONV_PAD_LEFT = 2
CONV_PAD_RIGHT = CONV_W - 1 - CONV_PAD_LEFT
RG_LRU_C = 8.0
D_FF = 5632
N_EXPERTS = 8
TOP_K = 2
D_FF_EXPERT = 7168
N_DENSE = (DEPTH + 1) // 2
N_MOE = DEPTH // 2
N_MOD = 6
EPS = 1e-6

kernel_name = "hybrid_fnet_rglru_diffusion_step"


def rms_norm(x, g):
    xf = x.astype(jnp.float32)
    y = xf * lax.rsqrt(jnp.mean(xf * xf, axis=-1, keepdims=True) + EPS)
    return (y * g.astype(jnp.float32)).astype(x.dtype)


def adaln_params(cond, w_mod, b_mod):
    m = jnp.matmul(jax.nn.silu(cond), w_mod) + b_mod
    return [t[..., None, :] for t in jnp.split(m, N_MOD, axis=-1)]


def fourier_mix(u, w_f, rows):
    b, n, _ = u.shape
    uf = u.astype(jnp.float32)
    if rows is None:
        uf = uf.reshape(b, n, N_FOURIER_HEADS, DH_FOURIER)
        axes = (1, 3)
    else:
        uf = uf.reshape(b, rows, GRID_W, N_FOURIER_HEADS, DH_FOURIER)
        axes = (1, 2, 4)
    mixed = jnp.fft.fftn(uf, axes=axes, norm="ortho").real.reshape(b, n, N_FOURIER_HEADS, DH_FOURIER)
    y = jnp.einsum("bnhd,hde->bnhe", mixed, w_f.astype(jnp.float32))
    return y.reshape(b, n, D_FOURIER)


def centred_conv(x, w, bias):
    n = x.shape[1]
    xp = jnp.pad(x, ((0, 0), (CONV_PAD_LEFT, CONV_PAD_RIGHT), (0, 0)))
    y = bias
    for k in range(CONV_W):
        y = y + xp[:, k:k + n, :] * w[k]
    return y


def linear_scan(a, bx, h0):
    def combine(p, q):
        a1, b1 = p
        a2, b2 = q
        return a1 * a2, a2 * b1 + b2
    a_cum, b_cum = lax.associative_scan(combine, (a, bx), axis=1)
    return a_cum * h0[:, None, :] + b_cum


def rg_lru(x, w_r, b_r, w_i, b_i, lam, h0):
    b, n, _ = x.shape
    xh = x.reshape(b, n, N_LRU_HEADS, DH_LRU)
    r = jax.nn.sigmoid(jnp.einsum("bnhd,hde->bnhe", xh, w_r).reshape(b, n, D_LRU) + b_r)
    i = jax.nn.sigmoid(jnp.einsum("bnhd,hde->bnhe", xh, w_i).reshape(b, n, D_LRU) + b_i)
    log_a = -RG_LRU_C * r * jax.nn.softplus(-lam)
    a = jnp.exp(log_a)
    mult = jnp.sqrt(-jnp.expm1(2.0 * log_a))
    return linear_scan(a, mult * (i * x), h0)


def mixer(h, rows, h0, w_in, w_f, gn_f, conv_w, conv_b, w_r, b_r, w_i, b_i, lam, gn_l, w_out):
    f32 = lambda t: t.astype(jnp.float32)
    u = jnp.matmul(h, w_in)
    u_f = u[..., :D_FOURIER]
    u_x = u[..., D_FOURIER:D_FOURIER + D_LRU]
    u_g = u[..., D_FOURIER + D_LRU:]
    y_f = rms_norm(fourier_mix(u_f, w_f, rows), gn_f).astype(h.dtype)
    xc = centred_conv(f32(u_x), f32(conv_w), f32(conv_b))
    h0f = f32(h0)
    h_fwd = rg_lru(xc, f32(w_r[0]), f32(b_r[0]), f32(w_i[0]), f32(b_i[0]), f32(lam[0]), h0f[:, 0])
    h_bwd = jnp.flip(rg_lru(jnp.flip(xc, 1), f32(w_r[1]), f32(b_r[1]), f32(w_i[1]), f32(b_i[1]),
                            f32(lam[1]), h0f[:, 1]), 1)
    y_l = (h_fwd + h_bwd) * jax.nn.gelu(f32(u_g))
    y_l = rms_norm(y_l, gn_l).astype(h.dtype)
    out = jnp.matmul(jnp.concatenate([y_f, y_l], axis=-1), w_out)
    return out, h_fwd, h_bwd


def swiglu(x, wg, wu, wd):
    return jnp.matmul(jax.nn.silu(jnp.matmul(x, wg)) * jnp.matmul(x, wu), wd)


def moe(x, router_w, router_b, wg, wu, wd):
    b, n, d = x.shape
    x2 = x.reshape(b * n, d)
    logits = jnp.matmul(x2.astype(jnp.float32), router_w.astype(jnp.float32)) + router_b.astype(jnp.float32)
    top_v, top_i = lax.top_k(logits, TOP_K)
    gates = jax.nn.softmax(top_v, axis=-1)
    combine = jnp.sum(jax.nn.one_hot(top_i, N_EXPERTS, dtype=jnp.float32) * gates[..., None], axis=1)
    out = jnp.zeros((b * n, d), jnp.float32)
    for e in range(N_EXPERTS):
        out = out + combine[:, e:e + 1] * swiglu(x2, wg[e], wu[e], wd[e])
    return out.astype(x.dtype).reshape(b, n, d)


def layer(x, l, cond, rows, h0, p):
    sh1, sc1, g1, sh2, sc2, g2 = adaln_params(cond, p["w_mod"][l], p["b_mod"][l])
    hm = rms_norm(x, p["norm1_g"][l]) * (1 + sc1) + sh1
    out, h_fwd, h_bwd = mixer(hm, rows, h0, p["w_in"][l], p["w_fourier"][l], p["gn_fourier_g"][l],
                              p["conv_w"][l], p["conv_b"][l], p["w_r"][l], p["b_r"][l],
                              p["w_i"][l], p["b_i"][l], p["lam"][l], p["gn_lru_g"][l], p["w_out"][l])
    x = x + g1 * out
    hm = rms_norm(x, p["norm2_g"][l]) * (1 + sc2) + sh2
    m = l // 2
    if l % 2 == 0:
        ff = swiglu(hm, p["ffn_w_gate"][m], p["ffn_w_up"][m], p["ffn_w_down"][m])
    else:
        ff = moe(hm, p["router_w"][m], p["router_b"][m], p["moe_w_gate"][m], p["moe_w_up"][m],
                 p["moe_w_down"][m])
    x = x + g2 * ff
    return x, h_fwd, h_bwd


def setup_inputs(seed: int = 0) -> dict:
    key = jax.random.key(seed)
    ks = jax.random.split(key, 32)
    f32 = jnp.float32

    def nrm(k, shape, scale):
        return jax.random.normal(k, shape, f32) * scale

    u = jax.random.uniform(ks[14], (DEPTH, 2, D_LRU), f32, minval=0.9, maxval=0.999)
    return {
        "x_prompt": nrm(ks[0], (BATCH, SEQ, D_MODEL), 1.0),
        "x_sample": nrm(ks[1], (DEC_BATCH, DEC_SEQ, D_MODEL), 1.0),
        "state_lru": nrm(ks[2], (DEC_BATCH, DEPTH, 2, D_LRU), 1.0),
        "c": nrm(ks[3], (DEC_BATCH, D_MODEL), 1.0),
        "c_ctx": nrm(ks[4], (D_MODEL,), 1.0),
        "norm1_g": 1.0 + nrm(ks[5], (DEPTH, D_MODEL), 0.02),
        "norm2_g": 1.0 + nrm(ks[6], (DEPTH, D_MODEL), 0.02),
        "w_mod": nrm(ks[7], (DEPTH, D_MODEL, N_MOD * D_MODEL), 0.5 * D_MODEL ** -0.5),
        "b_mod": nrm(ks[8], (DEPTH, N_MOD * D_MODEL), 0.02),
        "w_in": nrm(ks[9], (DEPTH, D_MODEL, D_IN), D_MODEL ** -0.5),
        "w_fourier": nrm(ks[10], (DEPTH, N_FOURIER_HEADS, DH_FOURIER, DH_FOURIER), DH_FOURIER ** -0.5),
        "gn_fourier_g": 1.0 + nrm(ks[11], (DEPTH, D_FOURIER), 0.02),
        "conv_w": nrm(ks[12], (DEPTH, CONV_W, D_LRU), CONV_W ** -0.5),
        "conv_b": nrm(ks[13], (DEPTH, D_LRU), 0.02),
        "w_r": nrm(ks[15], (DEPTH, 2, N_LRU_HEADS, DH_LRU, DH_LRU), DH_LRU ** -0.5),
        "b_r": nrm(ks[16], (DEPTH, 2, D_LRU), 0.02),
        "w_i": nrm(ks[17], (DEPTH, 2, N_LRU_HEADS, DH_LRU, DH_LRU), DH_LRU ** -0.5),
        "b_i": nrm(ks[18], (DEPTH, 2, D_LRU), 0.02),
        "lam": jnp.log(u) - jnp.log1p(-u),
        "gn_lru_g": 1.0 + nrm(ks[19], (DEPTH, D_LRU), 0.02),
        "w_out": nrm(ks[20], (DEPTH, D_MIX, D_MODEL), D_MIX ** -0.5),
        "ffn_w_gate": nrm(ks[21], (N_DENSE, D_MODEL, D_FF), D_MODEL ** -0.5),
        "ffn_w_up": nrm(ks[22], (N_DENSE, D_MODEL, D_FF), D_MODEL ** -0.5),
        "ffn_w_down": nrm(ks[23], (N_DENSE, D_FF, D_MODEL), D_FF ** -0.5),
        "router_w": nrm(ks[24], (N_MOE, D_MODEL, N_EXPERTS), D_MODEL ** -0.5),
        "router_b": nrm(ks[25], (N_MOE, N_EXPERTS), 0.01),
        "moe_w_gate": nrm(ks[26], (N_MOE, N_EXPERTS, D_MODEL, D_FF_EXPERT), D_MODEL ** -0.5),
        "moe_w_up": nrm(ks[27], (N_MOE, N_EXPERTS, D_MODEL, D_FF_EXPERT), D_MODEL ** -0.5),
        "moe_w_down": nrm(ks[28], (N_MOE, N_EXPERTS, D_FF_EXPERT, D_MODEL), D_FF_EXPERT ** -0.5),
        "final_g": 1.0 + nrm(ks[29], (D_MODEL,), 0.02),
    }


def reference(x_prompt, x_sample, state_lru, c, c_ctx, norm1_g, norm2_g, w_mod, b_mod, w_in,
              w_fourier, gn_fourier_g, conv_w, conv_b, w_r, b_r, w_i, b_i, lam, gn_lru_g, w_out,
              ffn_w_gate, ffn_w_up, ffn_w_down, router_w, router_b, moe_w_gate, moe_w_up,
              moe_w_down, final_g):
    p = dict(norm1_g=norm1_g, norm2_g=norm2_g, w_mod=w_mod, b_mod=b_mod, w_in=w_in,
             w_fourier=w_fourier, gn_fourier_g=gn_fourier_g, conv_w=conv_w, conv_b=conv_b,
             w_r=w_r, b_r=b_r, w_i=w_i, b_i=b_i, lam=lam, gn_lru_g=gn_lru_g, w_out=w_out,
             ffn_w_gate=ffn_w_gate, ffn_w_up=ffn_w_up, ffn_w_down=ffn_w_down,
             router_w=router_w, router_b=router_b, moe_w_gate=moe_w_gate,
             moe_w_up=moe_w_up, moe_w_down=moe_w_down)

    x = x_prompt
    h0_ctx = jnp.zeros((x.shape[0], 2, D_LRU), jnp.float32)
    ctx_states = []
    for l in range(DEPTH):
        x, h_fwd, h_bwd = layer(x, l, c_ctx, None, h0_ctx, p)
        ctx_states.append(jnp.stack([h_fwd[:, -1], h_bwd[:, 0]], axis=1))
    y_prompt = rms_norm(x, final_g)
    new_state_lru = jnp.stack(ctx_states, axis=1).astype(state_lru.dtype)

    xs = x_sample
    rows = xs.shape[1] // GRID_W
    for l in range(DEPTH):
        xs, _, _ = layer(xs, l, c, rows, state_lru[:, l], p)
    y_sample = rms_norm(xs, final_g)

    return (y_prompt, y_sample, new_state_lru)
```

```python
import functools
import math

import jax
import jax.numpy as jnp
import numpy as np
from jax import lax
from jax.experimental import pallas as pl
from jax.experimental.pallas import tpu as pltpu

f32 = jnp.float32
bf16 = jnp.bfloat16
i32 = jnp.int32

GRID_W = 64
N_HEADS = 4
CONV_W = 4
RG_LRU_C = 8.0
N_MOD = 6
TOP_K = 2
EPS = 1e-6

LANES = 128
SUBLANES = 8
VMEM_LIMIT_BYTES = 56 * 1024 * 1024

TM = 512
TF = 512
LRU_CHUNK = 256
TROW = 256
FOURIER_TQ = 512


def _cparams(*sem):
    return pltpu.CompilerParams(dimension_semantics=sem, vmem_limit_bytes=VMEM_LIMIT_BYTES)


def _rms(x, g):
    return x * lax.rsqrt(jnp.mean(x * x, axis=-1, keepdims=True) + EPS) * g


def _sigmoid(x):
    return 1.0 / (1.0 + jnp.exp(-x))


def _const_spec(shape):
    nd = len(shape)
    return pl.BlockSpec(shape, lambda *_: (0,) * nd, pipeline_mode=pl.Buffered(1))


def _mod_kernel(ct_ref, w_ref, b_ref, o_ref, *, n_cond, kc):
    d = ct_ref.shape[0]
    tn = w_ref.shape[2]
    o_ref[...] = jnp.zeros(o_ref.shape, f32)
    for r in range(n_cond):
        def body(k, acc, r=r):
            k0 = pl.multiple_of(k * kc, kc)
            c = ct_ref[pl.ds(k0, kc), r:r + 1]
            s = c * _sigmoid(c)
            w = w_ref[0, pl.ds(k0, kc), :]
            return acc + (w * s).reshape(kc // SUBLANES, SUBLANES, tn).sum(axis=0)
        acc = lax.fori_loop(0, d // kc, body, jnp.zeros((SUBLANES, tn), f32))
        o_ref[0, r:r + 1, :] = jnp.sum(acc, axis=0, keepdims=True) + b_ref[0]


def _mods(cond_t, w_mod, b_mod, n_cond):
    depth, d, n6 = w_mod.shape
    ncp = cond_t.shape[1]
    tn = 768
    return pl.pallas_call(
        functools.partial(_mod_kernel, n_cond=n_cond, kc=64),
        grid=(depth, n6 // tn),
        in_specs=[pl.BlockSpec((d, ncp), lambda l, j: (0, 0)),
                  pl.BlockSpec((1, d, tn), lambda l, j: (l, 0, j)),
                  pl.BlockSpec((1, 1, tn), lambda l, j: (l, 0, j))],
        out_specs=pl.BlockSpec((1, ncp, tn), lambda l, j: (l, 0, j)),
        out_shape=jax.ShapeDtypeStruct((depth, ncp, n6), f32),
        compiler_params=_cparams("arbitrary", "arbitrary"),
        name="adaln_mods",
    )(cond_t, w_mod, b_mod.reshape(depth, 1, n6))


def _ab_kernel(cs_ref, wf_ref, o_ref):
    dh = wf_ref.shape[1]
    w = wf_ref[0]
    o_ref[0, :, :dh] = jnp.dot(cs_ref[0], w, precision=lax.Precision.HIGHEST,
                               preferred_element_type=f32).astype(bf16)
    o_ref[0, :, dh:] = jnp.dot(cs_ref[1], w, precision=lax.Precision.HIGHEST,
                               preferred_element_type=f32).astype(bf16)


def _fold_channel_dft(cs, w_f):
    nh, dh, _ = w_f.shape
    return pl.pallas_call(
        _ab_kernel,
        grid=(nh,),
        in_specs=[pl.BlockSpec((2, dh, dh), lambda h: (0, 0, 0)),
                  pl.BlockSpec((1, dh, dh), lambda h: (h, 0, 0))],
        out_specs=pl.BlockSpec((1, dh, 2 * dh), lambda h: (h, 0, 0)),
        out_shape=jax.ShapeDtypeStruct((nh, dh, 2 * dh), bf16),
        compiler_params=_cparams("arbitrary"),
        name="fold_channel_dft",
    )(cs, w_f)


def _inproj_kernel(x_ref, sh_ref, sc_ref, g_ref, w_ref, ab_ref, pa_ref, pb_ref, ux_ref, ug_ref,
                   *, d_f, d_l):
    h = _rms(x_ref[...], g_ref[...]) * (1.0 + sc_ref[0]) + sh_ref[0]
    u = jnp.dot(h.astype(bf16), w_ref[...], preferred_element_type=f32)
    nh = ab_ref.shape[0]
    dh = d_f // nh
    for hh in range(nh):
        sl = slice(hh * dh, (hh + 1) * dh)
        p = jnp.dot(u[:, sl].astype(bf16), ab_ref[hh], preferred_element_type=f32)
        pa_ref[:, sl] = p[:, :dh].astype(bf16)
        pb_ref[:, sl] = p[:, dh:].astype(bf16)
    ux_ref[...] = u[:, d_f:d_f + d_l]
    ug_ref[...] = u[:, d_f + d_l:]


def _in_proj(x, mods_l, g, w_in, ab, cond_of_tile, d_f, d_l):
    t, d = x.shape
    d_in = w_in.shape[1]
    nh, dh, _ = ab.shape
    mod = lambda j: pl.BlockSpec((1, 1, d), lambda i: (cond_of_tile(i), 0, j))
    tok = lambda n: pl.BlockSpec((TM, n), lambda i: (i, 0))
    return pl.pallas_call(
        functools.partial(_inproj_kernel, d_f=d_f, d_l=d_l),
        grid=(t // TM,),
        in_specs=[tok(d), mod(0), mod(1), _const_spec((1, d)), _const_spec((d, d_in)),
                  _const_spec((nh, dh, 2 * dh))],
        out_specs=[tok(d_f), tok(d_f), tok(d_l), tok(d_l)],
        out_shape=[jax.ShapeDtypeStruct((t, d_f), bf16), jax.ShapeDtypeStruct((t, d_f), bf16),
                   jax.ShapeDtypeStruct((t, d_l), f32), jax.ShapeDtypeStruct((t, d_l), f32)],
        compiler_params=_cparams("arbitrary"),
        name="in_proj",
    )(x, mods_l, mods_l, g, w_in, ab)


def _fourier_ctx_kernel(pa_ref, pb_ref, c_ref, ms_ref, o_ref):
    for b in range(pa_ref.shape[0]):
        y = jnp.dot(c_ref[...], pa_ref[b], preferred_element_type=f32)
        y = y + jnp.dot(ms_ref[...], pb_ref[b], preferred_element_type=f32)
        o_ref[b] = y.astype(bf16)


def _fourier_ctx(pa, pb, cpos, mspos, n_batch, seq):
    t, d_f = pa.shape
    nb = 4
    view = (t // seq, seq, d_f)
    blk = pl.BlockSpec((nb, seq, d_f), lambda g: (g, 0, 0))
    return pl.pallas_call(
        _fourier_ctx_kernel,
        grid=(n_batch // nb,),
        in_specs=[blk, blk, _const_spec((seq, seq)), _const_spec((seq, seq))],
        out_specs=blk,
        out_shape=jax.ShapeDtypeStruct((n_batch, seq, d_f), bf16),
        compiler_params=_cparams("arbitrary"),
        name="fourier_pos_ctx",
    )(pa.reshape(view), pb.reshape(view), cpos, mspos)


def _fourier_lat_kernel(pa_ref, pb_ref, c_ref, ms_ref, o_ref):
    y = jnp.dot(c_ref[...], pa_ref[0], preferred_element_type=f32)
    y = y + jnp.dot(ms_ref[...], pb_ref[0], preferred_element_type=f32)
    o_ref[0] = y.astype(bf16)


def _fourier_lat(pa, pb, c2, ms2, n_batch, seq, batch0):
    t, d_f = pa.shape
    view = (t // seq, seq, d_f)
    tq = min(FOURIER_TQ, seq)
    pblk = pl.BlockSpec((1, seq, d_f), lambda b, q: (batch0 + b, 0, 0))
    cblk = pl.BlockSpec((tq, seq), lambda b, q: (q, 0))
    return pl.pallas_call(
        _fourier_lat_kernel,
        grid=(n_batch, seq // tq),
        in_specs=[pblk, pblk, cblk, cblk],
        out_specs=pl.BlockSpec((1, tq, d_f), lambda b, q: (b, q, 0)),
        out_shape=jax.ShapeDtypeStruct((n_batch, seq, d_f), bf16),
        compiler_params=_cparams("arbitrary", "arbitrary"),
        name="fourier_pos_lat",
    )(pa.reshape(view), pb.reshape(view), c2, ms2)


def _gelu_tanh(x):
    return x * (0.5 * (1.0 + jnp.tanh(math.sqrt(2.0 / math.pi) * (x + 0.044715 * (x * x * x)))))


def _lru_kernel(*refs, direction, combine, nb, lc, ch, pitch, n_chunks):
    n_slab = ch // LANES
    (ux, hp, hn, cw, cb, wg, bg, lam, h0) = refs[:9]
    if combine:
        (hf, ug, y_out, st_out) = refs[9:13]
    else:
        (hf_out, st_out) = refs[9:11]
    a_scr = refs[-2 * n_slab - 1:-n_slab - 1]
    b_scr = refs[-n_slab - 1:-1]
    car = refs[-1]
    c = pl.program_id(2)
    cc = c if direction == 0 else n_chunks - 1 - c

    @pl.when(c == 0)
    def _():
        car[...] = h0[...]

    row = lax.broadcasted_iota(i32, (lc, ch), 0)
    z = -lam[...]
    softplus = jnp.maximum(z, 0.0) + jnp.log1p(jnp.exp(-jnp.abs(z)))
    for b in range(nb):
        x = ux[b]
        zero = jnp.zeros((1, ch), f32)
        p6 = jnp.where(cc > 0, hp[b, 6:7, :], zero)
        p7 = jnp.where(cc > 0, hp[b, 7:8, :], zero)
        n0 = jnp.where(cc < n_chunks - 1, hn[b, 0:1, :], zero)
        xm1 = jnp.where(row == 0, p7, pltpu.roll(x, 1, 0))
        xm2 = jnp.where(row == 0, p6, jnp.where(row == 1, p7, pltpu.roll(x, 2, 0)))
        xp1 = jnp.where(row == lc - 1, n0, pltpu.roll(x, lc - 1, 0))
        xc = cb[...] + xm2 * cw[0:1, :] + xm1 * cw[1:2, :] + x * cw[2:3, :] + xp1 * cw[3:4, :]
        gates = jnp.dot(xc.astype(bf16), wg[0], preferred_element_type=f32) + bg[0]
        r = _sigmoid(gates[:, :ch])
        i = _sigmoid(gates[:, ch:])
        log_a = (-RG_LRU_C * r) * softplus
        a = jnp.exp(log_a)
        th = jnp.tanh(log_a)
        mult = jnp.sqrt((-2.0 * th) / (1.0 - th))
        bx = mult * (i * xc)
        for s in range(n_slab):
            a_scr[s][b * pitch:b * pitch + lc, :] = a[:, s * LANES:(s + 1) * LANES]
            b_scr[s][b * pitch:b * pitch + lc, :] = bx[:, s * LANES:(s + 1) * LANES]

    def step(k, hs):
        t = k if direction == 0 else lc - 1 - k
        out = []
        for s in range(n_slab):
            idx = pl.ds(t, nb, stride=pitch)
            h = a_scr[s][idx, :] * hs[s] + b_scr[s][idx, :]
            b_scr[s][idx, :] = h
            out.append(h)
        return tuple(out)

    hs = lax.fori_loop(0, lc, step, tuple(car[:, s * LANES:(s + 1) * LANES] for s in range(n_slab)),
                       unroll=8)
    for s in range(n_slab):
        car[:, s * LANES:(s + 1) * LANES] = hs[s]
    st_out[...] = car[...]

    for b in range(nb):
        for s in range(n_slab):
            sl = slice(s * LANES, (s + 1) * LANES)
            h = b_scr[s][b * pitch:b * pitch + lc, :]
            if combine:
                y_out[b, :, sl] = ((hf[b, :, sl] + h) * _gelu_tanh(ug[b, :, sl])).astype(bf16)
            else:
                hf_out[b, :, sl] = h


def _lru_pass(direction, ux, ug, hf, conv_w, conv_b, wg, bg, lam, h0, n_batch, seq, batch0, nb):
    t, d_l = ux.shape
    ch = d_l // N_HEADS
    lc = min(LRU_CHUNK, seq)
    n_chunks = seq // lc
    pitch = lc + SUBLANES
    view = (t // seq, seq, d_l)
    assert batch0 % nb == 0 and n_batch % nb == 0
    g0 = batch0 // nb
    combine = direction == 1
    pos = (lambda c: c) if direction == 0 else (lambda c: n_chunks - 1 - c)
    l8 = lc // SUBLANES
    n8 = seq // SUBLANES
    main = pl.BlockSpec((nb, lc, ch), lambda g, h, c: (g0 + g, pos(c), h))
    prev = pl.BlockSpec((nb, SUBLANES, ch), lambda g, h, c: (g0 + g, jnp.maximum(pos(c) * l8 - 1, 0), h))
    nxt = pl.BlockSpec((nb, SUBLANES, ch), lambda g, h, c: (g0 + g, jnp.minimum((pos(c) + 1) * l8, n8 - 1), h))
    local = pl.BlockSpec((nb, lc, ch), lambda g, h, c: (g, pos(c), h))
    head_row = lambda rows: pl.BlockSpec((rows, ch), lambda g, h, c: (0, h))
    state = pl.BlockSpec((nb, ch), lambda g, h, c: (g, h))
    in_specs = [main, prev, nxt, head_row(CONV_W), head_row(1),
                pl.BlockSpec((1, ch, 2 * ch), lambda g, h, c: (h, 0, 0)),
                pl.BlockSpec((1, 1, 2 * ch), lambda g, h, c: (h, 0, 0)),
                head_row(1), state]
    ux3 = ux.reshape(view)
    args = [ux3, ux3, ux3, conv_w, conv_b, wg, bg, lam, h0]
    scratch = [pltpu.VMEM((nb * pitch, LANES), f32) for _ in range(2 * (ch // LANES))]
    scratch.append(pltpu.VMEM((nb, ch), f32))
    st_shape = jax.ShapeDtypeStruct((n_batch, d_l), f32)
    if combine:
        in_specs += [local, main]
        args += [hf, ug.reshape(view)]
    out_specs = [local, state]
    out_shape = [jax.ShapeDtypeStruct((n_batch, seq, d_l), bf16 if combine else f32), st_shape]
    kern = functools.partial(_lru_kernel, direction=direction, combine=combine,
                             nb=nb, lc=lc, ch=ch, pitch=pitch, n_chunks=n_chunks)
    return pl.pallas_call(
        kern,
        grid=(n_batch // nb, N_HEADS, n_chunks),
        in_specs=in_specs, out_specs=out_specs, out_shape=out_shape,
        scratch_shapes=scratch,
        compiler_params=_cparams("arbitrary", "arbitrary", "arbitrary"),
        name="rglru_dir%d" % direction,
    )(*args)


def _outproj_kernel(*refs, d_f, n_ctx_tiles, routed):
    (yf_c, yf_l, yl_c, yl_l, x, g1, sh2, sc2, gnf, gnl, n2g, wo) = refs[:12]
    if routed:
        (rw, rb, x1_out, hm_out, lg_out) = refs[12:]
    else:
        (x1_out, hm_out) = refs[12:]
    is_ctx = pl.program_id(0) < n_ctx_tiles
    yf = jnp.where(is_ctx, yf_c[...], yf_l[...])
    yl = jnp.where(is_ctx, yl_c[...], yl_l[...])
    yfn = _rms(yf.astype(f32), gnf[...]).astype(bf16)
    yln = _rms(yl.astype(f32), gnl[...]).astype(bf16)
    out = jnp.dot(yfn, wo[:d_f, :], preferred_element_type=f32)
    out = out + jnp.dot(yln, wo[d_f:, :], preferred_element_type=f32)
    x1 = x[...] + g1[0] * out
    x1_out[...] = x1
    hm = _rms(x1, n2g[...]) * (1.0 + sc2[0]) + sh2[0]
    hm_out[...] = hm.astype(hm_out.dtype)
    if routed:
        ne = rw.shape[1] // 2
        hi = hm.astype(bf16)
        lo = (hm - hi.astype(f32)).astype(bf16)
        big = jnp.dot(hi, rw[...], preferred_element_type=f32)
        small = jnp.dot(lo, rw[:, :ne], preferred_element_type=f32)
        lg_out[...] = big[:, :ne] + (big[:, ne:] + small) + rb[...]


def _out_proj(yf_c, yf_l, yl_c, yl_l, x, mods_l, gnf, gnl, n2g, w_out, cond_of_tile, router):
    t, d = x.shape
    d_f = yf_c.shape[1]
    d_l = yl_c.shape[1]
    nct = yf_c.shape[0] // TM
    routed = router is not None
    mod = lambda j: pl.BlockSpec((1, 1, d), lambda i: (cond_of_tile(i), 0, j))
    tok = lambda n: pl.BlockSpec((TM, n), lambda i: (i, 0))
    ctx = lambda n: pl.BlockSpec((TM, n), lambda i: (jnp.minimum(i, nct - 1), 0))
    lat = lambda n: pl.BlockSpec((TM, n), lambda i: (jnp.maximum(i - nct, 0), 0))
    in_specs = [ctx(d_f), lat(d_f), ctx(d_l), lat(d_l), tok(d), mod(2), mod(3), mod(4), _const_spec((1, d_f)),
                _const_spec((1, d_l)), _const_spec((1, d)), _const_spec((d_f + d_l, d))]
    args = [yf_c, yf_l, yl_c, yl_l, x, mods_l, mods_l, mods_l, gnf, gnl, n2g, w_out]
    out_specs = [tok(d), tok(d)]
    out_shape = [jax.ShapeDtypeStruct((t, d), f32), jax.ShapeDtypeStruct((t, d), f32 if routed else bf16)]
    if routed:
        rw, rb = router
        in_specs += [_const_spec(rw.shape), _const_spec(rb.shape)]
        args += [rw, rb]
        out_specs.append(tok(LANES))
        out_shape.append(jax.ShapeDtypeStruct((t, LANES), f32))
    return pl.pallas_call(
        functools.partial(_outproj_kernel, d_f=d_f, n_ctx_tiles=nct, routed=routed),
        grid=(t // TM,),
        in_specs=in_specs, out_specs=out_specs, out_shape=out_shape,
        compiler_params=_cparams("arbitrary"),
        name="out_proj",
    )(*args)


def _ffn_kernel(*refs, final):
    if final:
        hm, x1, g2, wg, wu, wd, fg, o_ref, acc = refs
    else:
        hm, x1, g2, wg, wu, wd, o_ref, acc = refs
    f = pl.program_id(1)

    @pl.when(f == 0)
    def _():
        acc[...] = jnp.zeros(acc.shape, f32)

    xb = hm[...]
    g = jnp.dot(xb, wg[...], preferred_element_type=f32)
    u = jnp.dot(xb, wu[...], preferred_element_type=f32)
    h = ((g * _sigmoid(g)) * u).astype(bf16)
    acc[...] += jnp.dot(h, wd[...], preferred_element_type=f32)

    @pl.when(f == pl.num_programs(1) - 1)
    def _():
        y = x1[...] + g2[0] * acc[...]
        o_ref[...] = _rms(y, fg[...]) if final else y


def _ffn_dense(hm, x1, mods_l, wg, wu, wd, cond_of_tile, final_g):
    t, d = x1.shape
    dff = wg.shape[1]
    final = final_g is not None
    tok = pl.BlockSpec((TM, d), lambda i, f: (i, 0))
    in_specs = [tok, tok, pl.BlockSpec((1, 1, d), lambda i, f: (cond_of_tile(i), 0, 5)),
                pl.BlockSpec((d, TF), lambda i, f: (0, f)), pl.BlockSpec((d, TF), lambda i, f: (0, f)),
                pl.BlockSpec((TF, d), lambda i, f: (f, 0))]
    args = [hm, x1, mods_l, wg, wu, wd]
    if final:
        in_specs.append(pl.BlockSpec((1, d), lambda i, f: (0, 0)))
        args.append(final_g)
    return pl.pallas_call(
        functools.partial(_ffn_kernel, final=final),
        grid=(t // TM, dff // TF),
        in_specs=in_specs, out_specs=tok,
        out_shape=jax.ShapeDtypeStruct((t, d), f32),
        scratch_shapes=[pltpu.VMEM((TM, d), f32)],
        compiler_params=_cparams("arbitrary", "arbitrary"),
        name="ffn_dense",
    )(*args)


def _route_kernel(lg_ref, meta_ref, cnt_ref, run, *, n_exp):
    i = pl.program_id(0)
    tr = lg_ref.shape[0]

    @pl.when(i == 0)
    def _():
        run[...] = jnp.zeros(run.shape, f32)

    lane = lax.broadcasted_iota(i32, (tr, LANES), 1)
    lane_f = lane.astype(f32)
    neg = jnp.float32(-jnp.inf)
    lg = jnp.where(lane < n_exp, lg_ref[...], neg)
    m1 = jnp.max(lg, axis=1, keepdims=True)
    i1 = jnp.min(jnp.where(lg == m1, lane_f, float(LANES)), axis=1, keepdims=True)
    oh1 = lane_f == i1
    lg2 = jnp.where(oh1, neg, lg)
    m2 = jnp.max(lg2, axis=1, keepdims=True)
    i2 = jnp.min(jnp.where(lg2 == m2, lane_f, float(LANES)), axis=1, keepdims=True)
    oh2 = lane_f == i2
    e = jnp.exp(m2 - m1)
    g1 = 1.0 / (1.0 + e)
    g2 = e / (1.0 + e)
    sel = jnp.where(oh1 | oh2, 1.0, 0.0)
    rr = lax.broadcasted_iota(i32, (tr, tr), 0)
    rc = lax.broadcasted_iota(i32, (tr, tr), 1)
    tri = jnp.where(rc < rr, 1.0, 0.0).astype(bf16)
    pos = jnp.dot(tri, sel.astype(bf16), preferred_element_type=f32) + run[...]
    p1 = jnp.sum(jnp.where(oh1, pos, 0.0), axis=1, keepdims=True)
    p2 = jnp.sum(jnp.where(oh2, pos, 0.0), axis=1, keepdims=True)
    run[...] = run[...] + jnp.sum(sel, axis=0, keepdims=True)
    cnt_ref[...] = run[...]
    cols = (i1, i2, p1, p2, g1, g2)
    meta = jnp.zeros((tr, LANES), f32)
    for k, v in enumerate(cols):
        meta = jnp.where(lane == k, v, meta)
    meta_ref[...] = meta


def _route(logits, n_exp):
    t = logits.shape[0]
    return pl.pallas_call(
        functools.partial(_route_kernel, n_exp=n_exp),
        grid=(t // TM,),
        in_specs=[pl.BlockSpec((TM, LANES), lambda i: (i, 0))],
        out_specs=[pl.BlockSpec((TM, LANES), lambda i: (i, 0)), pl.BlockSpec((1, LANES), lambda i: (0, 0))],
        out_shape=[jax.ShapeDtypeStruct((t, LANES), f32), jax.ShapeDtypeStruct((1, LANES), f32)],
        scratch_shapes=[pltpu.VMEM((1, LANES), f32)],
        compiler_params=_cparams("arbitrary"),
        name="route_top2",
    )(logits)


def _row_copy(src, src_row, dst, dst_row, sem):
    return pltpu.make_async_copy(src.at[pl.ds(src_row, 1), :], dst.at[pl.ds(dst_row, 1), :], sem)


def _scatter_kernel(d1, d2, hm_ref, xs_in, xs_out, sem):
    del xs_in
    i = pl.program_id(0)
    n = hm_ref.shape[0]

    def issue(r, c):
        _row_copy(hm_ref, r, xs_out, d1[i * n + r], sem.at[0]).start()
        _row_copy(hm_ref, r, xs_out, d2[i * n + r], sem.at[0]).start()
        return c

    def drain(r, c):
        _row_copy(hm_ref, 0, xs_out, 0, sem.at[0]).wait()
        _row_copy(hm_ref, 0, xs_out, 0, sem.at[0]).wait()
        return c

    lax.fori_loop(0, n, issue, 0)
    lax.fori_loop(0, n, drain, 0)


def _scatter_rows(d1, d2, hm, xs_zero):
    t, d = hm.shape
    return pl.pallas_call(
        _scatter_kernel,
        grid_spec=pltpu.PrefetchScalarGridSpec(
            num_scalar_prefetch=2, grid=(t // TROW,),
            in_specs=[pl.BlockSpec((TROW, d), lambda i, a, b: (i, 0)), pl.BlockSpec(memory_space=pl.ANY)],
            out_specs=pl.BlockSpec(memory_space=pl.ANY),
            scratch_shapes=[pltpu.SemaphoreType.DMA((1,))]),
        out_shape=jax.ShapeDtypeStruct(xs_zero.shape, f32),
        input_output_aliases={3: 0},
        compiler_params=_cparams("arbitrary"),
        name="moe_scatter_rows",
    )(d1, d2, hm, xs_zero)


def _moe_kernel(te, nu, xs, wg, wu, wd, o_ref, xb, acc):
    del te
    j = pl.program_id(0)
    f = pl.program_id(1)
    valid = j < nu[0]

    @pl.when(valid & (f == 0))
    def _():
        xb[...] = xs[...].astype(bf16)
        acc[...] = jnp.zeros(acc.shape, f32)

    @pl.when(valid)
    def _():
        g = jnp.dot(xb[...], wg[0], preferred_element_type=f32)
        u = jnp.dot(xb[...], wu[0], preferred_element_type=f32)
        h = ((g * _sigmoid(g)) * u).astype(bf16)
        acc[...] += jnp.dot(h, wd[0], preferred_element_type=f32)

    @pl.when(f == pl.num_programs(1) - 1)
    def _():
        o_ref[...] = jnp.where(valid, acc[...], 0.0)


def _moe_grouped(tile_expert, n_used, xs, wg, wu, wd):
    r, d = xs.shape
    dff = wg.shape[2]
    nf = dff // TF
    row = lambda j, f, te, nu: (jnp.minimum(j, nu[0] - 1), 0)
    fidx = lambda j, f, nu: jnp.where(j < nu[0], f, nf - 1)
    return pl.pallas_call(
        _moe_kernel,
        grid_spec=pltpu.PrefetchScalarGridSpec(
            num_scalar_prefetch=2, grid=(r // TM, nf),
            in_specs=[pl.BlockSpec((TM, d), row),
                      pl.BlockSpec((1, d, TF), lambda j, f, te, nu: (te[j], 0, fidx(j, f, nu))),
                      pl.BlockSpec((1, d, TF), lambda j, f, te, nu: (te[j], 0, fidx(j, f, nu))),
                      pl.BlockSpec((1, TF, d), lambda j, f, te, nu: (te[j], fidx(j, f, nu), 0))],
            out_specs=pl.BlockSpec((TM, d), lambda j, f, te, nu: (j, 0)),
            scratch_shapes=[pltpu.VMEM((TM, d), bf16), pltpu.VMEM((TM, d), f32)]),
        out_shape=jax.ShapeDtypeStruct((r, d), f32),
        compiler_params=_cparams("arbitrary", "arbitrary"),
        name="moe_grouped_swiglu",
    )(tile_expert, n_used, xs, wg, wu, wd)


def _combine_kernel(*refs, final):
    if final:
        d1, d2, x1, g2, meta, fg, ys, o_ref, buf, sem = refs
    else:
        d1, d2, x1, g2, meta, ys, o_ref, buf, sem = refs
    i = pl.program_id(0)
    n = x1.shape[0]

    def issue(r, c):
        _row_copy(ys, d1[i * n + r], buf.at[0], r, sem.at[0]).start()
        _row_copy(ys, d2[i * n + r], buf.at[1], r, sem.at[0]).start()
        return c

    def drain(r, c):
        _row_copy(ys, 0, buf.at[0], 0, sem.at[0]).wait()
        _row_copy(ys, 0, buf.at[0], 0, sem.at[0]).wait()
        return c

    lax.fori_loop(0, n, issue, 0)
    lax.fori_loop(0, n, drain, 0)
    m = meta[...]
    ff = m[:, 4:5] * buf[0] + m[:, 5:6] * buf[1]
    y = x1[...] + g2[0] * ff
    o_ref[...] = _rms(y, fg[...]) if final else y


def _combine_rows(d1, d2, x1, mods_l, meta, ys, cond_of_tile_row, final_g):
    t, d = x1.shape
    final = final_g is not None
    tok = lambda n: pl.BlockSpec((TROW, n), lambda i, a, b: (i, 0))
    in_specs = [tok(d), pl.BlockSpec((1, 1, d), lambda i, a, b: (cond_of_tile_row(i), 0, 5)), tok(LANES)]
    args = [x1, mods_l, meta]
    if final:
        in_specs.append(pl.BlockSpec((1, d), lambda i, a, b: (0, 0)))
        args.append(final_g)
    in_specs.append(pl.BlockSpec(memory_space=pl.ANY))
    args.append(ys)
    return pl.pallas_call(
        functools.partial(_combine_kernel, final=final),
        grid_spec=pltpu.PrefetchScalarGridSpec(
            num_scalar_prefetch=2, grid=(t // TROW,),
            in_specs=in_specs, out_specs=tok(d),
            scratch_shapes=[pltpu.VMEM((2, TROW, d), f32), pltpu.SemaphoreType.DMA((1,))]),
        out_shape=jax.ShapeDtypeStruct((t, d), f32),
        compiler_params=_cparams("arbitrary"),
        name="moe_combine_rows",
    )(d1, d2, *args)


def _moe_ffn(hm, logits, x1, mods_l, wg, wu, wd, cond_of_tile_row, final_g):
    t, d = hm.shape
    n_exp = wg.shape[0]
    meta, counts = _route(logits, n_exp)
    cnt = counts[0, :n_exp].astype(i32)
    padded = ((cnt + TM - 1) // TM) * TM
    ends = jnp.cumsum(padded)
    off = ends - padded
    e1 = meta[:, 0].astype(i32)
    e2 = meta[:, 1].astype(i32)
    d1 = off[e1] + meta[:, 2].astype(i32)
    d2 = off[e2] + meta[:, 3].astype(i32)
    n_tiles = (TOP_K * t) // TM + n_exp
    tile_start = jnp.arange(n_tiles, dtype=i32) * TM
    n_used = (ends[-1] // TM).reshape(1).astype(i32)
    tile_expert = jnp.sum((tile_start[:, None] >= ends[None, :]).astype(i32), axis=1)
    tile_expert = jnp.where(tile_start < ends[-1], tile_expert, tile_expert[n_used[0] - 1])
    xs = _scatter_rows(d1, d2, hm, jnp.zeros((n_tiles * TM, d), f32))
    ys = _moe_grouped(tile_expert, n_used, xs, wg, wu, wd)
    return _combine_rows(d1, d2, x1, mods_l, meta, ys, cond_of_tile_row, final_g)


@functools.lru_cache(maxsize=None)
def _dft_tables(n_ctx, rows, grid_w, dh):
    def cos_sin(k, period):
        ang = 2.0 * np.pi * (k % period).astype(np.float64) / period
        return np.cos(ang), np.sin(ang)

    k = np.arange(dh)
    cc, sc = cos_sin(k[:, None] * k[None, :], dh)
    chan = np.stack([cc, sc]).astype(np.float32)
    t = np.arange(n_ctx)
    cp, sp = cos_sin(t[:, None] * t[None, :], n_ctx)
    s_ctx = 1.0 / math.sqrt(n_ctx * dh)
    n = np.arange(rows * grid_w)
    r, c = n // grid_w, n % grid_w
    period = rows * grid_w // math.gcd(rows, grid_w)
    phase = (r[:, None] * r[None, :]) * (period // rows) + (c[:, None] * c[None, :]) * (period // grid_w)
    c2, s2 = cos_sin(phase, period)
    s_lat = 1.0 / math.sqrt(rows * grid_w * dh)
    as_f32 = lambda a: np.asarray(a, np.float32)
    return chan, as_f32(cp * s_ctx), as_f32(-sp * s_ctx), as_f32(c2 * s_lat), as_f32(-s2 * s_lat)


def kernel(x_prompt, x_sample, state_lru, c, c_ctx, norm1_g, norm2_g, w_mod, b_mod, w_in, w_fourier,
           gn_fourier_g, conv_w, conv_b, w_r, b_r, w_i, b_i, lam, gn_lru_g, w_out, ffn_w_gate, ffn_w_up,
           ffn_w_down, router_w, router_b, moe_w_gate, moe_w_up, moe_w_down, final_g):
    nb_ctx, seq_ctx, d = x_prompt.shape
    nb_lat, seq_lat, _ = x_sample.shape
    depth = w_mod.shape[0]
    d_f = w_fourier.shape[1] * w_fourier.shape[2]
    d_l = lam.shape[2]
    dh = d_f // N_HEADS
    t_ctx = nb_ctx * seq_ctx
    t_lat = nb_lat * seq_lat
    assert t_ctx % TM == 0 and seq_lat % TM == 0 and t_ctx % seq_lat == 0 and seq_lat % GRID_W == 0
    assert d_l // N_HEADS == dh and dh % LANES == 0 and TM % TROW == 0

    def cond_of(tile_rows):
        return lambda i: jnp.where(i * tile_rows < t_ctx, 0,
                                   1 + jnp.maximum(i * tile_rows - t_ctx, 0) // seq_lat)

    cond_tm, cond_row = cond_of(TM), cond_of(TROW)

    n_cond = 1 + nb_lat
    ncp = -(-n_cond // SUBLANES) * SUBLANES
    cond = jnp.concatenate([c_ctx[None, :], c, jnp.zeros((ncp - n_cond, d), f32)], axis=0)
    mods = _mods(cond.T, w_mod, b_mod, n_cond).reshape(depth, ncp, 1, N_MOD * d)

    chan, cp, msp, c2, ms2 = _dft_tables(seq_ctx, seq_lat // GRID_W, GRID_W, dh)
    chan = jnp.asarray(chan)
    cp, msp, c2, ms2 = (jnp.asarray(a).astype(bf16) for a in (cp, msp, c2, ms2))

    x = jnp.concatenate([x_prompt.reshape(t_ctx, d), x_sample.reshape(t_lat, d)], axis=0)
    zeros_h0 = jnp.zeros((nb_ctx, d_l), f32)
    ctx_states = []
    for l in range(depth):
        mods_l = mods[l]
        ab = _fold_channel_dft(chan, w_fourier[l])
        pa, pb, ux, ug = _in_proj(x, mods_l, norm1_g[l][None, :], w_in[l].astype(bf16), ab, cond_tm, d_f, d_l)

        yf_c = _fourier_ctx(pa, pb, cp, msp, nb_ctx, seq_ctx).reshape(t_ctx, d_f)
        yf_l = _fourier_lat(pa, pb, c2, ms2, nb_lat, seq_lat, t_ctx // seq_lat).reshape(t_lat, d_f)

        def gate_w(dr):
            wgt = jnp.concatenate([w_r[l, dr], w_i[l, dr]], axis=-1).astype(bf16)
            bias = jnp.concatenate([b_r[l, dr].reshape(N_HEADS, 1, dh), b_i[l, dr].reshape(N_HEADS, 1, dh)], axis=-1)
            return wgt, bias

        yls = []
        states = []
        for (n_batch, seq, batch0, nb, h0) in (
                (nb_ctx, seq_ctx, 0, SUBLANES, (zeros_h0, zeros_h0)),
                (nb_lat, seq_lat, t_ctx // seq_lat, nb_lat,
                 (state_lru[:, l, 0], state_lru[:, l, 1]))):
            wg0, bg0 = gate_w(0)
            hf, st_f = _lru_pass(0, ux, None, None, conv_w[l], conv_b[l][None, :], wg0, bg0,
                                 lam[l, 0][None, :], h0[0], n_batch, seq, batch0, nb)
            wg1, bg1 = gate_w(1)
            yl, st_b = _lru_pass(1, ux, ug, hf, conv_w[l], conv_b[l][None, :], wg1, bg1,
                                 lam[l, 1][None, :], h0[1], n_batch, seq, batch0, nb)
            yls.append(yl.reshape(n_batch * seq, d_l))
            states.append((st_f, st_b))
        ctx_states.append(jnp.stack(states[0], axis=1))
        mixed = (yf_c, yf_l, yls[0], yls[1])

        m = l // 2
        last = l == depth - 1
        fg = final_g[None, :] if last else None
        if l % 2 == 0:
            x1, hm = _out_proj(*mixed, x, mods_l, gn_fourier_g[l][None, :], gn_lru_g[l][None, :],
                               norm2_g[l][None, :], w_out[l].astype(bf16), cond_tm, None)
            x = _ffn_dense(hm, x1, mods_l, ffn_w_gate[m].astype(bf16), ffn_w_up[m].astype(bf16),
                           ffn_w_down[m].astype(bf16), cond_tm, fg)
        else:
            n_exp = router_w.shape[2]
            rw = jnp.zeros((d, LANES), f32).at[:, :n_exp].set(router_w[m])
            rw_hi = rw.astype(bf16)
            rw_lo = (rw - rw_hi.astype(f32)).astype(bf16)
            rb = jnp.zeros((1, LANES), f32).at[0, :n_exp].set(router_b[m])
            x1, hm, logits = _out_proj(*mixed, x, mods_l, gn_fourier_g[l][None, :], gn_lru_g[l][None, :],
                                       norm2_g[l][None, :], w_out[l].astype(bf16), cond_tm,
                                       (jnp.concatenate([rw_hi, rw_lo], axis=1), rb))
            x = _moe_ffn(hm, logits, x1, mods_l, moe_w_gate[m].astype(bf16), moe_w_up[m].astype(bf16),
                         moe_w_down[m].astype(bf16), cond_row, fg)
    y_prompt = x[:t_ctx].reshape(nb_ctx, seq_ctx, d)
    y_sample = x[t_ctx:].reshape(nb_lat, seq_lat, d)
    new_state = jnp.stack(ctx_states, axis=1).astype(state_lru.dtype)
    return (y_prompt, y_sample, new_state)
```

```python
import functools
import math

import jax
import jax.numpy as jnp
import numpy as np
from jax import lax
from jax.experimental import pallas as pl
from jax.experimental.pallas import tpu as pltpu

f32 = jnp.float32
bf16 = jnp.bfloat16
i32 = jnp.int32

GRID_W = 64
N_HEADS = 4
CONV_W = 4
RG_LRU_C = 8.0
N_MOD = 6
TOP_K = 2
EPS = 1e-6

LANES = 128
SUBLANES = 8
VMEM_LIMIT_BYTES = 56 * 1024 * 1024

MOE_VMEM_LIMIT_BYTES = 60 * 1024 * 1024

TM = 512
TF = 512
TMM = 1024
MOE_SUB = 512
TFM = 256
LRU_CHUNK = 256
TROW = 256
FOURIER_TQ = 512


def _cparams(*sem):
    return pltpu.CompilerParams(dimension_semantics=sem, vmem_limit_bytes=VMEM_LIMIT_BYTES)


def _rms(x, g):
    return x * lax.rsqrt(jnp.mean(x * x, axis=-1, keepdims=True) + EPS) * g


def _sigmoid(x):
    return 1.0 / (1.0 + jnp.exp(-x))


def _const_spec(shape):
    nd = len(shape)
    return pl.BlockSpec(shape, lambda *_: (0,) * nd, pipeline_mode=pl.Buffered(1))


def _mod_kernel(ct_ref, w_ref, b_ref, o_ref, sb, *, n_cond, kc):
    d = ct_ref.shape[0]
    n_lt = w_ref.shape[2] // LANES

    @pl.when((pl.program_id(0) == 0) & (pl.program_id(1) == 0))
    def _():
        c = ct_ref[...]
        s = c * _sigmoid(c)
        for r in range(n_cond):
            sb[r] = jnp.broadcast_to(s[:, r:r + 1], (d, LANES))

    def body(k, accs):
        k0 = pl.multiple_of(k * kc, kc)
        ws = [w_ref[0, pl.ds(k0, kc), lt * LANES:(lt + 1) * LANES] for lt in range(n_lt)]
        out = []
        for r in range(n_cond):
            sr = sb[r, pl.ds(k0, kc), :]
            for lt in range(n_lt):
                part = (ws[lt] * sr).reshape(kc // SUBLANES, SUBLANES, LANES).sum(axis=0)
                out.append(accs[r * n_lt + lt] + part)
        return tuple(out)

    zero = jnp.zeros((SUBLANES, LANES), f32)
    accs = lax.fori_loop(0, d // kc, body, tuple(zero for _ in range(n_cond * n_lt)), unroll=2)
    o_ref[...] = jnp.zeros(o_ref.shape, f32)
    for r in range(n_cond):
        for lt in range(n_lt):
            sl = slice(lt * LANES, (lt + 1) * LANES)
            o_ref[0, r:r + 1, sl] = jnp.sum(accs[r * n_lt + lt], axis=0, keepdims=True) + b_ref[0, :, sl]


def _mods(cond_t, w_mod, b_mod, n_cond):
    depth, d, n6 = w_mod.shape
    ncp = cond_t.shape[1]
    tn = 768
    return pl.pallas_call(
        functools.partial(_mod_kernel, n_cond=n_cond, kc=16),
        grid=(depth, n6 // tn),
        in_specs=[pl.BlockSpec((d, ncp), lambda l, j: (0, 0)),
                  pl.BlockSpec((1, d, tn), lambda l, j: (l, 0, j)),
                  pl.BlockSpec((1, 1, tn), lambda l, j: (l, 0, j))],
        out_specs=pl.BlockSpec((1, ncp, tn), lambda l, j: (l, 0, j)),
        out_shape=jax.ShapeDtypeStruct((depth, ncp, n6), f32),
        scratch_shapes=[pltpu.VMEM((n_cond, d, LANES), f32)],
        compiler_params=_cparams("arbitrary", "arbitrary"),
        name="adaln_mods",
    )(cond_t, w_mod, b_mod.reshape(depth, 1, n6))


def _ab_kernel(cs_ref, wf_ref, o_ref):
    dh = wf_ref.shape[1]
    w = wf_ref[0]
    o_ref[0, :, :dh] = jnp.dot(cs_ref[0], w, precision=lax.Precision.HIGHEST,
                               preferred_element_type=f32).astype(bf16)
    o_ref[0, :, dh:] = jnp.dot(cs_ref[1], w, precision=lax.Precision.HIGHEST,
                               preferred_element_type=f32).astype(bf16)


def _fold_channel_dft(cs, w_f):
    nh, dh, _ = w_f.shape
    return pl.pallas_call(
        _ab_kernel,
        grid=(nh,),
        in_specs=[pl.BlockSpec((2, dh, dh), lambda h: (0, 0, 0)),
                  pl.BlockSpec((1, dh, dh), lambda h: (h, 0, 0))],
        out_specs=pl.BlockSpec((1, dh, 2 * dh), lambda h: (h, 0, 0)),
        out_shape=jax.ShapeDtypeStruct((nh, dh, 2 * dh), bf16),
        compiler_params=_cparams("arbitrary"),
        name="fold_channel_dft",
    )(cs, w_f)


def _stream_specs(x, n_cols, n_ctx_tiles, rows=TM):
    if not isinstance(x, tuple):
        return [pl.BlockSpec((rows, n_cols), lambda i, *_: (i, 0))], [x]
    ctx = pl.BlockSpec((rows, n_cols), lambda i, *_: (jnp.minimum(i, n_ctx_tiles - 1), 0))
    lat = pl.BlockSpec((rows, n_cols), lambda i, *_: (jnp.maximum(i - n_ctx_tiles, 0), 0))
    return [ctx, lat], list(x)


def _stream_value(refs, n_ctx_tiles):
    if len(refs) == 1:
        return refs[0][...]
    return jnp.where(pl.program_id(0) < n_ctx_tiles, refs[0][...], refs[1][...])


def _stream_store(refs, y, n_ctx_tiles):
    if len(refs) == 1:
        refs[0][...] = y
        return
    is_ctx = pl.program_id(0) < n_ctx_tiles

    @pl.when(is_ctx)
    def _():
        refs[0][...] = y

    @pl.when(jnp.logical_not(is_ctx))
    def _():
        refs[1][...] = y


def _inproj_kernel(*refs, d_f, d_l, n_x, n_ctx_tiles):
    x = _stream_value(refs[:n_x], n_ctx_tiles)
    (sh_ref, sc_ref, g_ref, w_ref, ab_ref, pa_ref, pb_ref, ux_ref, ug_ref) = refs[n_x:]
    h = _rms(x, g_ref[...]) * (1.0 + sc_ref[0]) + sh_ref[0]
    u = jnp.dot(h.astype(bf16), w_ref[...], preferred_element_type=f32)
    nh = ab_ref.shape[0]
    dh = d_f // nh
    for hh in range(nh):
        sl = slice(hh * dh, (hh + 1) * dh)
        p = jnp.dot(u[:, sl].astype(bf16), ab_ref[hh], preferred_element_type=f32)
        pa_ref[:, sl] = p[:, :dh].astype(bf16)
        pb_ref[:, sl] = p[:, dh:].astype(bf16)
    ux_ref[...] = u[:, d_f:d_f + d_l]
    ug_ref[...] = u[:, d_f + d_l:]


def _in_proj(x, t, n_ctx_tiles, mods_l, g, w_in, ab, cond_of_tile, d_f, d_l):
    d, d_in = w_in.shape
    nh, dh, _ = ab.shape
    mod = lambda j: pl.BlockSpec((1, 1, d), lambda i: (cond_of_tile(i), 0, j))
    tok = lambda n: pl.BlockSpec((TM, n), lambda i: (i, 0))
    x_specs, x_args = _stream_specs(x, d, n_ctx_tiles)
    return pl.pallas_call(
        functools.partial(_inproj_kernel, d_f=d_f, d_l=d_l, n_x=len(x_args), n_ctx_tiles=n_ctx_tiles),
        grid=(t // TM,),
        in_specs=x_specs + [mod(0), mod(1), _const_spec((1, d)), _const_spec((d, d_in)),
                            _const_spec((nh, dh, 2 * dh))],
        out_specs=[tok(d_f), tok(d_f), tok(d_l), tok(d_l)],
        out_shape=[jax.ShapeDtypeStruct((t, d_f), bf16), jax.ShapeDtypeStruct((t, d_f), bf16),
                   jax.ShapeDtypeStruct((t, d_l), f32), jax.ShapeDtypeStruct((t, d_l), f32)],
        compiler_params=_cparams("arbitrary"),
        name="in_proj",
    )(*x_args, mods_l, mods_l, g, w_in, ab)


def _fourier_ctx_kernel(pa_ref, pb_ref, c_ref, ms_ref, o_ref):
    for b in range(pa_ref.shape[0]):
        y = jnp.dot(c_ref[...], pa_ref[b], preferred_element_type=f32)
        y = y + jnp.dot(ms_ref[...], pb_ref[b], preferred_element_type=f32)
        o_ref[b] = y.astype(bf16)


def _fourier_ctx(pa, pb, cpos, mspos, n_batch, seq):
    t, d_f = pa.shape
    nb = 4
    view = (t // seq, seq, d_f)
    blk = pl.BlockSpec((nb, seq, d_f), lambda g: (g, 0, 0))
    return pl.pallas_call(
        _fourier_ctx_kernel,
        grid=(n_batch // nb,),
        in_specs=[blk, blk, _const_spec((seq, seq)), _const_spec((seq, seq))],
        out_specs=blk,
        out_shape=jax.ShapeDtypeStruct((n_batch, seq, d_f), bf16),
        compiler_params=_cparams("arbitrary"),
        name="fourier_pos_ctx",
    )(pa.reshape(view), pb.reshape(view), cpos, mspos)


def _fourier_lat_kernel(pa_ref, pb_ref, c_ref, ms_ref, o_ref):
    y = jnp.dot(c_ref[...], pa_ref[0], preferred_element_type=f32)
    y = y + jnp.dot(ms_ref[...], pb_ref[0], preferred_element_type=f32)
    o_ref[0] = y.astype(bf16)


def _fourier_lat(pa, pb, c2, ms2, n_batch, seq, batch0):
    t, d_f = pa.shape
    view = (t // seq, seq, d_f)
    tq = min(FOURIER_TQ, seq)
    pblk = pl.BlockSpec((1, seq, d_f), lambda b, q: (batch0 + b, 0, 0))
    cblk = pl.BlockSpec((tq, seq), lambda b, q: (q, 0))
    return pl.pallas_call(
        _fourier_lat_kernel,
        grid=(n_batch, seq // tq),
        in_specs=[pblk, pblk, cblk, cblk],
        out_specs=pl.BlockSpec((1, tq, d_f), lambda b, q: (b, q, 0)),
        out_shape=jax.ShapeDtypeStruct((n_batch, seq, d_f), bf16),
        compiler_params=_cparams("arbitrary", "arbitrary"),
        name="fourier_pos_lat",
    )(pa.reshape(view), pb.reshape(view), c2, ms2)


def _gelu_tanh(x):
    return x * (0.5 * (1.0 + jnp.tanh(math.sqrt(2.0 / math.pi) * (x + 0.044715 * (x * x * x)))))


def _lru_kernel(*refs, direction, combine, nb, lc, ch, pitch, n_chunks):
    n_slab = ch // LANES
    (ux, hp, hn, cw, cb, wg, bg, lam, h0) = refs[:9]
    if combine:
        (hf, ug, y_out, st_out) = refs[9:13]
    else:
        (hf_out, st_out) = refs[9:11]
    a_scr = refs[-2 * n_slab - 1:-n_slab - 1]
    b_scr = refs[-n_slab - 1:-1]
    car = refs[-1]
    c = pl.program_id(2)
    cc = c if direction == 0 else n_chunks - 1 - c

    @pl.when(c == 0)
    def _():
        car[...] = h0[...]

    row = lax.broadcasted_iota(i32, (lc, ch), 0)
    z = -lam[...]
    softplus = jnp.maximum(z, 0.0) + jnp.log1p(jnp.exp(-jnp.abs(z)))
    for b in range(nb):
        x = ux[b]
        zero = jnp.zeros((1, ch), f32)
        p6 = jnp.where(cc > 0, hp[b, 6:7, :], zero)
        p7 = jnp.where(cc > 0, hp[b, 7:8, :], zero)
        n0 = jnp.where(cc < n_chunks - 1, hn[b, 0:1, :], zero)
        xm1 = jnp.where(row == 0, p7, pltpu.roll(x, 1, 0))
        xm2 = jnp.where(row == 0, p6, jnp.where(row == 1, p7, pltpu.roll(x, 2, 0)))
        xp1 = jnp.where(row == lc - 1, n0, pltpu.roll(x, lc - 1, 0))
        xc = cb[...] + xm2 * cw[0:1, :] + xm1 * cw[1:2, :] + x * cw[2:3, :] + xp1 * cw[3:4, :]
        gates = jnp.dot(xc.astype(bf16), wg[0], preferred_element_type=f32) + bg[0]
        r = _sigmoid(gates[:, :ch])
        i = _sigmoid(gates[:, ch:])
        log_a = (-RG_LRU_C * r) * softplus
        a = jnp.exp(log_a)
        th = jnp.tanh(log_a)
        mult = jnp.sqrt((-2.0 * th) / (1.0 - th))
        bx = mult * (i * xc)
        for s in range(n_slab):
            a_scr[s][b * pitch:b * pitch + lc, :] = a[:, s * LANES:(s + 1) * LANES]
            b_scr[s][b * pitch:b * pitch + lc, :] = bx[:, s * LANES:(s + 1) * LANES]

    def step(k, hs):
        t = k if direction == 0 else lc - 1 - k
        out = []
        for s in range(n_slab):
            idx = pl.ds(t, nb, stride=pitch)
            h = a_scr[s][idx, :] * hs[s] + b_scr[s][idx, :]
            b_scr[s][idx, :] = h
            out.append(h)
        return tuple(out)

    hs = lax.fori_loop(0, lc, step, tuple(car[:, s * LANES:(s + 1) * LANES] for s in range(n_slab)),
                       unroll=8)
    for s in range(n_slab):
        car[:, s * LANES:(s + 1) * LANES] = hs[s]
    st_out[...] = car[...]

    for b in range(nb):
        for s in range(n_slab):
            sl = slice(s * LANES, (s + 1) * LANES)
            h = b_scr[s][b * pitch:b * pitch + lc, :]
            if combine:
                y_out[b, :, sl] = ((hf[b, :, sl] + h) * _gelu_tanh(ug[b, :, sl])).astype(bf16)
            else:
                hf_out[b, :, sl] = h


def _lru_pass(direction, ux, ug, hf, conv_w, conv_b, wg, bg, lam, h0, n_batch, seq, batch0, nb):
    t, d_l = ux.shape
    ch = d_l // N_HEADS
    lc = min(LRU_CHUNK, seq)
    n_chunks = seq // lc
    pitch = lc + SUBLANES
    view = (t // seq, seq, d_l)
    assert batch0 % nb == 0 and n_batch % nb == 0
    g0 = batch0 // nb
    combine = direction == 1
    pos = (lambda c: c) if direction == 0 else (lambda c: n_chunks - 1 - c)
    l8 = lc // SUBLANES
    n8 = seq // SUBLANES
    main = pl.BlockSpec((nb, lc, ch), lambda g, h, c: (g0 + g, pos(c), h))
    prev = pl.BlockSpec((nb, SUBLANES, ch), lambda g, h, c: (g0 + g, jnp.maximum(pos(c) * l8 - 1, 0), h))
    nxt = pl.BlockSpec((nb, SUBLANES, ch), lambda g, h, c: (g0 + g, jnp.minimum((pos(c) + 1) * l8, n8 - 1), h))
    local = pl.BlockSpec((nb, lc, ch), lambda g, h, c: (g, pos(c), h))
    head_row = lambda rows: pl.BlockSpec((rows, ch), lambda g, h, c: (0, h))
    state = pl.BlockSpec((nb, ch), lambda g, h, c: (g, h))
    in_specs = [main, prev, nxt, head_row(CONV_W), head_row(1),
                pl.BlockSpec((1, ch, 2 * ch), lambda g, h, c: (h, 0, 0)),
                pl.BlockSpec((1, 1, 2 * ch), lambda g, h, c: (h, 0, 0)),
                head_row(1), state]
    ux3 = ux.reshape(view)
    args = [ux3, ux3, ux3, conv_w, conv_b, wg, bg, lam, h0]
    scratch = [pltpu.VMEM((nb * pitch, LANES), f32) for _ in range(2 * (ch // LANES))]
    scratch.append(pltpu.VMEM((nb, ch), f32))
    st_shape = jax.ShapeDtypeStruct((n_batch, d_l), f32)
    if combine:
        in_specs += [local, main]
        args += [hf, ug.reshape(view)]
    out_specs = [local, state]
    out_shape = [jax.ShapeDtypeStruct((n_batch, seq, d_l), bf16 if combine else f32), st_shape]
    kern = functools.partial(_lru_kernel, direction=direction, combine=combine,
                             nb=nb, lc=lc, ch=ch, pitch=pitch, n_chunks=n_chunks)
    return pl.pallas_call(
        kern,
        grid=(n_batch // nb, N_HEADS, n_chunks),
        in_specs=in_specs, out_specs=out_specs, out_shape=out_shape,
        scratch_shapes=scratch,
        compiler_params=_cparams("arbitrary", "arbitrary", "arbitrary"),
        name="rglru_dir%d" % direction,
    )(*args)


def _outproj_kernel(*refs, d_f, n_ctx_tiles, n_x, routed):
    yf = _stream_value(refs[0:2], n_ctx_tiles)
    yl = _stream_value(refs[2:4], n_ctx_tiles)
    x = _stream_value(refs[4:4 + n_x], n_ctx_tiles)
    (g1, sh2, sc2, gnf, gnl, n2g, wo) = refs[4 + n_x:11 + n_x]
    if routed:
        (rw, rb, x1_out, hm_out, lg_out) = refs[11 + n_x:]
    else:
        (x1_out, hm_out) = refs[11 + n_x:]
    yfn = _rms(yf.astype(f32), gnf[...]).astype(bf16)
    yln = _rms(yl.astype(f32), gnl[...]).astype(bf16)
    out = jnp.dot(yfn, wo[:d_f, :], preferred_element_type=f32)
    out = out + jnp.dot(yln, wo[d_f:, :], preferred_element_type=f32)
    x1 = x + g1[0] * out
    x1_out[...] = x1
    hm = _rms(x1, n2g[...]) * (1.0 + sc2[0]) + sh2[0]
    hm_out[...] = hm.astype(hm_out.dtype)
    if routed:
        ne = rw.shape[1] // 2
        hi = hm.astype(bf16)
        lo = (hm - hi.astype(f32)).astype(bf16)
        big = jnp.dot(hi, rw[...], preferred_element_type=f32)
        small = jnp.dot(lo, rw[:, :ne], preferred_element_type=f32)
        lg_out[...] = big[:, :ne] + (big[:, ne:] + small) + rb[...]


def _out_proj(yf, yl, x, t, mods_l, gnf, gnl, n2g, w_out, cond_of_tile, router):
    d = w_out.shape[1]
    d_f = yf[0].shape[1]
    d_l = yl[0].shape[1]
    nct = yf[0].shape[0] // TM
    routed = router is not None
    mod = lambda j: pl.BlockSpec((1, 1, d), lambda i: (cond_of_tile(i), 0, j))
    tok = lambda n: pl.BlockSpec((TM, n), lambda i: (i, 0))
    yf_specs, yf_args = _stream_specs(yf, d_f, nct)
    yl_specs, yl_args = _stream_specs(yl, d_l, nct)
    x_specs, x_args = _stream_specs(x, d, nct)
    in_specs = yf_specs + yl_specs + x_specs + [
        mod(2), mod(3), mod(4), _const_spec((1, d_f)), _const_spec((1, d_l)), _const_spec((1, d)),
        _const_spec((d_f + d_l, d))]
    args = yf_args + yl_args + x_args + [mods_l, mods_l, mods_l, gnf, gnl, n2g, w_out]
    out_specs = [tok(d), tok(d)]
    out_shape = [jax.ShapeDtypeStruct((t, d), f32), jax.ShapeDtypeStruct((t, d), f32 if routed else bf16)]
    if routed:
        rw, rb = router
        in_specs += [_const_spec(rw.shape), _const_spec(rb.shape)]
        args += [rw, rb]
        out_specs.append(tok(LANES))
        out_shape.append(jax.ShapeDtypeStruct((t, LANES), f32))
    return pl.pallas_call(
        functools.partial(_outproj_kernel, d_f=d_f, n_ctx_tiles=nct, n_x=len(x_args), routed=routed),
        grid=(t // TM,),
        in_specs=in_specs, out_specs=out_specs, out_shape=out_shape,
        compiler_params=_cparams("arbitrary"),
        name="out_proj",
    )(*args)


def _ffn_kernel(*refs, final, n_ctx_tiles):
    if final:
        hm, x1, g2, wg, wu, wd, fg = refs[:7]
        o_refs = refs[7:9]
    else:
        hm, x1, g2, wg, wu, wd = refs[:6]
        o_refs = refs[6:7]
    acc = refs[-1]
    f = pl.program_id(1)

    @pl.when(f == 0)
    def _():
        acc[...] = jnp.zeros(acc.shape, f32)

    xb = hm[...]
    g = jnp.dot(xb, wg[...], preferred_element_type=f32)
    u = jnp.dot(xb, wu[...], preferred_element_type=f32)
    h = ((g * _sigmoid(g)) * u).astype(bf16)
    acc[...] += jnp.dot(h, wd[...], preferred_element_type=f32)

    @pl.when(f == pl.num_programs(1) - 1)
    def _():
        y = x1[...] + g2[0] * acc[...]
        _stream_store(o_refs, _rms(y, fg[...]) if final else y, n_ctx_tiles)


def _final_out(t, t_ctx, d, rows, final):
    shapes = (jax.ShapeDtypeStruct((t_ctx, d), f32), jax.ShapeDtypeStruct((t - t_ctx, d), f32))
    specs, _ = _stream_specs(shapes if final else None, d, t_ctx // rows, rows)
    return specs, (list(shapes) if final else [jax.ShapeDtypeStruct((t, d), f32)])


def _ffn_dense(hm, x1, mods_l, wg, wu, wd, cond_of_tile, final_g, t_ctx):
    t, d = x1.shape
    dff = wg.shape[1]
    final = final_g is not None
    tok = pl.BlockSpec((TM, d), lambda i, f: (i, 0))
    in_specs = [tok, tok, pl.BlockSpec((1, 1, d), lambda i, f: (cond_of_tile(i), 0, 5)),
                pl.BlockSpec((d, TF), lambda i, f: (0, f)), pl.BlockSpec((d, TF), lambda i, f: (0, f)),
                pl.BlockSpec((TF, d), lambda i, f: (f, 0))]
    args = [hm, x1, mods_l, wg, wu, wd]
    if final:
        in_specs.append(pl.BlockSpec((1, d), lambda i, f: (0, 0)))
        args.append(final_g)
    out_specs, out_shape = _final_out(t, t_ctx, d, TM, final)
    out = pl.pallas_call(
        functools.partial(_ffn_kernel, final=final, n_ctx_tiles=t_ctx // TM),
        grid=(t // TM, dff // TF),
        in_specs=in_specs, out_specs=out_specs, out_shape=out_shape,
        scratch_shapes=[pltpu.VMEM((TM, d), f32)],
        compiler_params=_cparams("arbitrary", "arbitrary"),
        name="ffn_dense",
    )(*args)
    return tuple(out) if final else out[0]


def _route_kernel(lg_ref, meta_ref, cnt_ref, run, *, n_exp):
    i = pl.program_id(0)
    tr = lg_ref.shape[0]

    @pl.when(i == 0)
    def _():
        run[...] = jnp.zeros(run.shape, f32)

    lane = lax.broadcasted_iota(i32, (tr, LANES), 1)
    lane_f = lane.astype(f32)
    neg = jnp.float32(-jnp.inf)
    lg = jnp.where(lane < n_exp, lg_ref[...], neg)
    m1 = jnp.max(lg, axis=1, keepdims=True)
    i1 = jnp.min(jnp.where(lg == m1, lane_f, float(LANES)), axis=1, keepdims=True)
    oh1 = lane_f == i1
    lg2 = jnp.where(oh1, neg, lg)
    m2 = jnp.max(lg2, axis=1, keepdims=True)
    i2 = jnp.min(jnp.where(lg2 == m2, lane_f, float(LANES)), axis=1, keepdims=True)
    oh2 = lane_f == i2
    e = jnp.exp(m2 - m1)
    g1 = 1.0 / (1.0 + e)
    g2 = e / (1.0 + e)
    sel = jnp.where(oh1 | oh2, 1.0, 0.0)
    rr = lax.broadcasted_iota(i32, (tr, tr), 0)
    rc = lax.broadcasted_iota(i32, (tr, tr), 1)
    tri = jnp.where(rc < rr, 1.0, 0.0).astype(bf16)
    pos = jnp.dot(tri, sel.astype(bf16), preferred_element_type=f32) + run[...]
    p1 = jnp.sum(jnp.where(oh1, pos, 0.0), axis=1, keepdims=True)
    p2 = jnp.sum(jnp.where(oh2, pos, 0.0), axis=1, keepdims=True)
    run[...] = run[...] + jnp.sum(sel, axis=0, keepdims=True)
    cnt_ref[...] = run[...]
    cols = (i1, i2, p1, p2, g1, g2)
    meta = jnp.zeros((tr, LANES), f32)
    for k, v in enumerate(cols):
        meta = jnp.where(lane == k, v, meta)
    meta_ref[...] = meta


def _route(logits, n_exp):
    t = logits.shape[0]
    return pl.pallas_call(
        functools.partial(_route_kernel, n_exp=n_exp),
        grid=(t // TM,),
        in_specs=[pl.BlockSpec((TM, LANES), lambda i: (i, 0))],
        out_specs=[pl.BlockSpec((TM, LANES), lambda i: (i, 0)), pl.BlockSpec((1, LANES), lambda i: (0, 0))],
        out_shape=[jax.ShapeDtypeStruct((t, LANES), f32), jax.ShapeDtypeStruct((1, LANES), f32)],
        scratch_shapes=[pltpu.VMEM((1, LANES), f32)],
        compiler_params=_cparams("arbitrary"),
        name="route_top2",
    )(logits)


def _row_copy(src, src_row, dst, dst_row, sem):
    return pltpu.make_async_copy(src.at[pl.ds(src_row, 1), :], dst.at[pl.ds(dst_row, 1), :], sem)


def _scatter_kernel(d1, d2, hm_ref, xs_in, xs_out, sem):
    del xs_in
    i = pl.program_id(0)
    n = hm_ref.shape[0]

    def issue(r, c):
        _row_copy(hm_ref, r, xs_out, d1[i * n + r], sem.at[0]).start()
        _row_copy(hm_ref, r, xs_out, d2[i * n + r], sem.at[0]).start()
        return c

    def drain(r, c):
        _row_copy(hm_ref, 0, xs_out, 0, sem.at[0]).wait()
        _row_copy(hm_ref, 0, xs_out, 0, sem.at[0]).wait()
        return c

    lax.fori_loop(0, n, issue, 0)
    lax.fori_loop(0, n, drain, 0)


def _scatter_rows(d1, d2, hm, xs_zero):
    t, d = hm.shape
    return pl.pallas_call(
        _scatter_kernel,
        grid_spec=pltpu.PrefetchScalarGridSpec(
            num_scalar_prefetch=2, grid=(t // TROW,),
            in_specs=[pl.BlockSpec((TROW, d), lambda i, a, b: (i, 0)), pl.BlockSpec(memory_space=pl.ANY)],
            out_specs=pl.BlockSpec(memory_space=pl.ANY),
            scratch_shapes=[pltpu.SemaphoreType.DMA((1,))]),
        out_shape=jax.ShapeDtypeStruct(xs_zero.shape, f32),
        input_output_aliases={3: 0},
        compiler_params=_cparams("arbitrary"),
        name="moe_scatter_rows",
    )(d1, d2, hm, xs_zero)


def _moe_kernel(te, nu, nv, xs, wg, wu, wd, o_ref, xb):
    del te, nu
    j = pl.program_id(0)
    f = pl.program_id(1)
    rows = nv[j]

    @pl.when(f == 0)
    def _():
        o_ref[...] = jnp.zeros(o_ref.shape, f32)

    @pl.when((f == 0) & (rows > 0))
    def _():
        xb[...] = xs[...].astype(bf16)

    for p in range(xb.shape[0] // MOE_SUB):
        @pl.when(rows > p * MOE_SUB)
        def _(p=p):
            sl = slice(p * MOE_SUB, (p + 1) * MOE_SUB)
            x = xb[sl, :]
            g = jnp.dot(x, wg[0].astype(bf16), preferred_element_type=f32)
            u = jnp.dot(x, wu[0].astype(bf16), preferred_element_type=f32)
            h = ((g * _sigmoid(g)) * u).astype(bf16)
            o_ref[sl, :] += jnp.dot(h, wd[0].astype(bf16), preferred_element_type=f32)


def _moe_grouped(tile_expert, n_used, tile_rows, xs, wg, wu, wd, m):
    r, d = xs.shape
    dff = wg.shape[3]
    nf = dff // TFM
    row = lambda j, f, te, nu, nv: (jnp.minimum(j, nu[0] - 1), 0)
    fidx = lambda j, f, nu: jnp.where(j < nu[0], f, nf - 1)
    return pl.pallas_call(
        _moe_kernel,
        grid_spec=pltpu.PrefetchScalarGridSpec(
            num_scalar_prefetch=3, grid=(r // TMM, nf),
            in_specs=[pl.BlockSpec((TMM, d), row),
                      pl.BlockSpec((None, 1, d, TFM), lambda j, f, te, nu, nv: (m, te[j], 0, fidx(j, f, nu))),
                      pl.BlockSpec((None, 1, d, TFM), lambda j, f, te, nu, nv: (m, te[j], 0, fidx(j, f, nu))),
                      pl.BlockSpec((None, 1, TFM, d), lambda j, f, te, nu, nv: (m, te[j], fidx(j, f, nu), 0))],
            out_specs=pl.BlockSpec((TMM, d), lambda j, f, te, nu, nv: (j, 0)),
            scratch_shapes=[pltpu.VMEM((TMM, d), bf16)]),
        out_shape=jax.ShapeDtypeStruct((r, d), f32),
        compiler_params=pltpu.CompilerParams(dimension_semantics=("arbitrary", "arbitrary"),
                                             vmem_limit_bytes=MOE_VMEM_LIMIT_BYTES),
        name="moe_grouped_swiglu",
    )(tile_expert, n_used, tile_rows, xs, wg, wu, wd)


def _combine_kernel(*refs, final, n_ctx_tiles):
    if final:
        d1, d2, x1, g2, meta, fg, ys = refs[:7]
        o_refs = refs[7:9]
    else:
        d1, d2, x1, g2, meta, ys = refs[:6]
        o_refs = refs[6:7]
    buf, sem = refs[-2:]
    i = pl.program_id(0)
    n = x1.shape[0]

    def issue(r, c):
        _row_copy(ys, d1[i * n + r], buf.at[0], r, sem.at[0]).start()
        _row_copy(ys, d2[i * n + r], buf.at[1], r, sem.at[0]).start()
        return c

    def drain(r, c):
        _row_copy(ys, 0, buf.at[0], 0, sem.at[0]).wait()
        _row_copy(ys, 0, buf.at[0], 0, sem.at[0]).wait()
        return c

    lax.fori_loop(0, n, issue, 0)
    lax.fori_loop(0, n, drain, 0)
    m = meta[...]
    ff = m[:, 4:5] * buf[0] + m[:, 5:6] * buf[1]
    y = x1[...] + g2[0] * ff
    _stream_store(o_refs, _rms(y, fg[...]) if final else y, n_ctx_tiles)


def _combine_rows(d1, d2, x1, mods_l, meta, ys, cond_of_tile_row, final_g, t_ctx):
    t, d = x1.shape
    final = final_g is not None
    tok = lambda n: pl.BlockSpec((TROW, n), lambda i, a, b: (i, 0))
    in_specs = [tok(d), pl.BlockSpec((1, 1, d), lambda i, a, b: (cond_of_tile_row(i), 0, 5)), tok(LANES)]
    args = [x1, mods_l, meta]
    if final:
        in_specs.append(pl.BlockSpec((1, d), lambda i, a, b: (0, 0)))
        args.append(final_g)
    in_specs.append(pl.BlockSpec(memory_space=pl.ANY))
    args.append(ys)
    out_specs, out_shape = _final_out(t, t_ctx, d, TROW, final)
    out = pl.pallas_call(
        functools.partial(_combine_kernel, final=final, n_ctx_tiles=t_ctx // TROW),
        grid_spec=pltpu.PrefetchScalarGridSpec(
            num_scalar_prefetch=2, grid=(t // TROW,),
            in_specs=in_specs, out_specs=out_specs,
            scratch_shapes=[pltpu.VMEM((2, TROW, d), f32), pltpu.SemaphoreType.DMA((1,))]),
        out_shape=out_shape,
        compiler_params=_cparams("arbitrary"),
        name="moe_combine_rows",
    )(d1, d2, *args)
    return tuple(out) if final else out[0]


def _moe_ffn(hm, logits, x1, mods_l, wg, wu, wd, m, cond_of_tile_row, final_g, t_ctx):
    t, d = hm.shape
    n_exp = wg.shape[1]
    assert (TOP_K * t) % TMM == 0 and TMM % MOE_SUB == 0
    meta, counts = _route(logits, n_exp)
    cnt = counts[0, :n_exp].astype(i32)
    padded = ((cnt + TMM - 1) // TMM) * TMM
    ends = jnp.cumsum(padded)
    off = ends - padded
    e1 = meta[:, 0].astype(i32)
    e2 = meta[:, 1].astype(i32)
    d1 = off[e1] + meta[:, 2].astype(i32)
    d2 = off[e2] + meta[:, 3].astype(i32)
    n_tiles = (TOP_K * t) // TMM + n_exp
    tile_start = jnp.arange(n_tiles, dtype=i32) * TMM
    n_used = (ends[-1] // TMM).reshape(1).astype(i32)
    tile_expert = jnp.minimum(jnp.sum((tile_start[:, None] >= ends[None, :]).astype(i32), axis=1), n_exp - 1)
    tile_rows = jnp.clip((off + cnt)[tile_expert] - tile_start, 0, TMM)
    tile_rows = jnp.where(tile_start < ends[-1], tile_rows, 0)
    tile_expert = jnp.where(tile_start < ends[-1], tile_expert, tile_expert[n_used[0] - 1])
    xs = _scatter_rows(d1, d2, hm, jnp.zeros((n_tiles * TMM, d), f32))
    ys = _moe_grouped(tile_expert, n_used, tile_rows, xs, wg, wu, wd, m)
    return _combine_rows(d1, d2, x1, mods_l, meta, ys, cond_of_tile_row, final_g, t_ctx)


@functools.lru_cache(maxsize=None)
def _dft_tables(n_ctx, rows, grid_w, dh):
    def cos_sin(k, period):
        ang = 2.0 * np.pi * (k % period).astype(np.float64) / period
        return np.cos(ang), np.sin(ang)

    k = np.arange(dh)
    cc, sc = cos_sin(k[:, None] * k[None, :], dh)
    chan = np.stack([cc, sc]).astype(np.float32)
    t = np.arange(n_ctx)
    cp, sp = cos_sin(t[:, None] * t[None, :], n_ctx)
    s_ctx = 1.0 / math.sqrt(n_ctx * dh)
    n = np.arange(rows * grid_w)
    r, c = n // grid_w, n % grid_w
    period = rows * grid_w // math.gcd(rows, grid_w)
    phase = (r[:, None] * r[None, :]) * (period // rows) + (c[:, None] * c[None, :]) * (period // grid_w)
    c2, s2 = cos_sin(phase, period)
    s_lat = 1.0 / math.sqrt(rows * grid_w * dh)
    as_f32 = lambda a: np.asarray(a, np.float32)
    return chan, as_f32(cp * s_ctx), as_f32(-sp * s_ctx), as_f32(c2 * s_lat), as_f32(-s2 * s_lat)


def kernel(x_prompt, x_sample, state_lru, c, c_ctx, norm1_g, norm2_g, w_mod, b_mod, w_in, w_fourier,
           gn_fourier_g, conv_w, conv_b, w_r, b_r, w_i, b_i, lam, gn_lru_g, w_out, ffn_w_gate, ffn_w_up,
           ffn_w_down, router_w, router_b, moe_w_gate, moe_w_up, moe_w_down, final_g):
    nb_ctx, seq_ctx, d = x_prompt.shape
    nb_lat, seq_lat, _ = x_sample.shape
    depth = w_mod.shape[0]
    d_f = w_fourier.shape[1] * w_fourier.shape[2]
    d_l = lam.shape[2]
    dh = d_f // N_HEADS
    t_ctx = nb_ctx * seq_ctx
    t_lat = nb_lat * seq_lat
    assert t_ctx % TM == 0 and seq_lat % TM == 0 and t_ctx % seq_lat == 0 and seq_lat % GRID_W == 0
    assert d_l // N_HEADS == dh and dh % LANES == 0 and TM % TROW == 0

    def cond_of(tile_rows):
        return lambda i: jnp.where(i * tile_rows < t_ctx, 0,
                                   1 + jnp.maximum(i * tile_rows - t_ctx, 0) // seq_lat)

    cond_tm, cond_row = cond_of(TM), cond_of(TROW)

    n_cond = 1 + nb_lat
    ncp = -(-n_cond // SUBLANES) * SUBLANES
    cond = jnp.concatenate([c_ctx[None, :], c, jnp.zeros((ncp - n_cond, d), f32)], axis=0)
    mods = _mods(cond.T, w_mod, b_mod, n_cond).reshape(depth, ncp, 1, N_MOD * d)

    chan, cp, msp, c2, ms2 = _dft_tables(seq_ctx, seq_lat // GRID_W, GRID_W, dh)
    chan = jnp.asarray(chan)
    cp, msp, c2, ms2 = (jnp.asarray(a).astype(bf16) for a in (cp, msp, c2, ms2))

    t = t_ctx + t_lat
    x = (x_prompt.reshape(t_ctx, d), x_sample.reshape(t_lat, d))
    zeros_h0 = jnp.zeros((nb_ctx, d_l), f32)
    ctx_states = []
    for l in range(depth):
        mods_l = mods[l]
        ab = _fold_channel_dft(chan, w_fourier[l])
        pa, pb, ux, ug = _in_proj(x, t, t_ctx // TM, mods_l, norm1_g[l][None, :], w_in[l].astype(bf16), ab,
                                  cond_tm, d_f, d_l)

        yf_c = _fourier_ctx(pa, pb, cp, msp, nb_ctx, seq_ctx).reshape(t_ctx, d_f)
        yf_l = _fourier_lat(pa, pb, c2, ms2, nb_lat, seq_lat, t_ctx // seq_lat).reshape(t_lat, d_f)

        def gate_w(dr):
            wgt = jnp.concatenate([w_r[l, dr], w_i[l, dr]], axis=-1).astype(bf16)
            bias = jnp.concatenate([b_r[l, dr].reshape(N_HEADS, 1, dh), b_i[l, dr].reshape(N_HEADS, 1, dh)], axis=-1)
            return wgt, bias

        yls = []
        states = []
        for (n_batch, seq, batch0, nb, h0) in (
                (nb_ctx, seq_ctx, 0, SUBLANES, (zeros_h0, zeros_h0)),
                (nb_lat, seq_lat, t_ctx // seq_lat, nb_lat,
                 (state_lru[:, l, 0], state_lru[:, l, 1]))):
            wg0, bg0 = gate_w(0)
            hf, st_f = _lru_pass(0, ux, None, None, conv_w[l], conv_b[l][None, :], wg0, bg0,
                                 lam[l, 0][None, :], h0[0], n_batch, seq, batch0, nb)
            wg1, bg1 = gate_w(1)
            yl, st_b = _lru_pass(1, ux, ug, hf, conv_w[l], conv_b[l][None, :], wg1, bg1,
                                 lam[l, 1][None, :], h0[1], n_batch, seq, batch0, nb)
            yls.append(yl.reshape(n_batch * seq, d_l))
            states.append((st_f, st_b))
        ctx_states.append(jnp.stack(states[0], axis=1))
        mixed = ((yf_c, yf_l), (yls[0], yls[1]))

        m = l // 2
        last = l == depth - 1
        fg = final_g[None, :] if last else None
        if l % 2 == 0:
            x1, hm = _out_proj(*mixed, x, t, mods_l, gn_fourier_g[l][None, :], gn_lru_g[l][None, :],
                               norm2_g[l][None, :], w_out[l].astype(bf16), cond_tm, None)
            x = _ffn_dense(hm, x1, mods_l, ffn_w_gate[m].astype(bf16), ffn_w_up[m].astype(bf16),
                           ffn_w_down[m].astype(bf16), cond_tm, fg, t_ctx)
        else:
            n_exp = router_w.shape[2]
            rw = jnp.zeros((d, LANES), f32).at[:, :n_exp].set(router_w[m])
            rw_hi = rw.astype(bf16)
            rw_lo = (rw - rw_hi.astype(f32)).astype(bf16)
            rb = jnp.zeros((1, LANES), f32).at[0, :n_exp].set(router_b[m])
            x1, hm, logits = _out_proj(*mixed, x, t, mods_l, gn_fourier_g[l][None, :], gn_lru_g[l][None, :],
                                       norm2_g[l][None, :], w_out[l].astype(bf16), cond_tm,
                                       (jnp.concatenate([rw_hi, rw_lo], axis=1), rb))
            x = _moe_ffn(hm, logits, x1, mods_l, moe_w_gate, moe_w_up, moe_w_down, m, cond_row, fg, t_ctx)
    y_prompt = x[0].reshape(nb_ctx, seq_ctx, d)
    y_sample = x[1].reshape(nb_lat, seq_lat, d)
    new_state = jnp.stack(ctx_states, axis=1).astype(state_lru.dtype)
    return (y_prompt, y_sample, new_state)
```

```python
import functools
import math

import jax
import jax.numpy as jnp
import numpy as np
from jax import lax
from jax.experimental import pallas as pl
from jax.experimental.pallas import tpu as pltpu

f32 = jnp.float32
bf16 = jnp.bfloat16
i32 = jnp.int32

GRID_W = 64
N_HEADS = 4
CONV_W = 4
RG_LRU_C = 8.0
N_MOD = 6
TOP_K = 2
EPS = 1e-6

LANES = 128
SUBLANES = 8
VMEM_LIMIT_BYTES = 56 * 1024 * 1024

MOE_VMEM_LIMIT_BYTES = 60 * 1024 * 1024

TM = 512
TMD = 512
TFD = 512
TMM = 1024
MOE_SUB = 512
TFM = 256
MOE_XS_BUFFERS = 2
LRU_CHUNK = 256
SCAN_UNROLL = 8
TROW = 256
ROW_UNROLL = 8
FOURIER_TQ = 512


def _cparams(*sem):
    return pltpu.CompilerParams(dimension_semantics=sem, vmem_limit_bytes=VMEM_LIMIT_BYTES)


def _rms(x, g):
    return x * lax.rsqrt(jnp.mean(x * x, axis=-1, keepdims=True) + EPS) * g


def _sigmoid(x):
    return 1.0 / (1.0 + jnp.exp(-x))


def _const_spec(shape):
    nd = len(shape)
    return pl.BlockSpec(shape, lambda *_: (0,) * nd, pipeline_mode=pl.Buffered(1))


def _mod_kernel(ct_ref, w_ref, b_ref, o_ref, sb, *, n_cond, kc):
    d = ct_ref.shape[0]
    n_lt = w_ref.shape[2] // LANES

    @pl.when((pl.program_id(0) == 0) & (pl.program_id(1) == 0))
    def _():
        c = ct_ref[...]
        s = c * _sigmoid(c)
        for r in range(n_cond):
            sb[r] = jnp.broadcast_to(s[:, r:r + 1], (d, LANES))

    def body(k, accs):
        k0 = pl.multiple_of(k * kc, kc)
        ws = [w_ref[0, pl.ds(k0, kc), lt * LANES:(lt + 1) * LANES] for lt in range(n_lt)]
        out = []
        for r in range(n_cond):
            sr = sb[r, pl.ds(k0, kc), :]
            for lt in range(n_lt):
                part = (ws[lt] * sr).reshape(kc // SUBLANES, SUBLANES, LANES).sum(axis=0)
                out.append(accs[r * n_lt + lt] + part)
        return tuple(out)

    zero = jnp.zeros((SUBLANES, LANES), f32)
    accs = lax.fori_loop(0, d // kc, body, tuple(zero for _ in range(n_cond * n_lt)), unroll=2)
    o_ref[...] = jnp.zeros(o_ref.shape, f32)
    for r in range(n_cond):
        for lt in range(n_lt):
            sl = slice(lt * LANES, (lt + 1) * LANES)
            o_ref[0, r:r + 1, sl] = jnp.sum(accs[r * n_lt + lt], axis=0, keepdims=True) + b_ref[0, :, sl]


def _mods(cond_t, w_mod, b_mod, n_cond):
    depth, d, n6 = w_mod.shape
    ncp = cond_t.shape[1]
    tn = 768
    return pl.pallas_call(
        functools.partial(_mod_kernel, n_cond=n_cond, kc=16),
        grid=(depth, n6 // tn),
        in_specs=[pl.BlockSpec((d, ncp), lambda l, j: (0, 0)),
                  pl.BlockSpec((1, d, tn), lambda l, j: (l, 0, j)),
                  pl.BlockSpec((1, 1, tn), lambda l, j: (l, 0, j))],
        out_specs=pl.BlockSpec((1, ncp, tn), lambda l, j: (l, 0, j)),
        out_shape=jax.ShapeDtypeStruct((depth, ncp, n6), f32),
        scratch_shapes=[pltpu.VMEM((n_cond, d, LANES), f32)],
        compiler_params=_cparams("arbitrary", "arbitrary"),
        name="adaln_mods",
    )(cond_t, w_mod, b_mod.reshape(depth, 1, n6))


def _ab_kernel(cs_ref, wf_ref, o_ref):
    dh = wf_ref.shape[1]
    w = wf_ref[0]
    o_ref[0, :, :dh] = jnp.dot(cs_ref[0], w, precision=lax.Precision.HIGHEST,
                               preferred_element_type=f32).astype(bf16)
    o_ref[0, :, dh:] = jnp.dot(cs_ref[1], w, precision=lax.Precision.HIGHEST,
                               preferred_element_type=f32).astype(bf16)


def _fold_channel_dft(cs, w_f):
    nh, dh, _ = w_f.shape
    return pl.pallas_call(
        _ab_kernel,
        grid=(nh,),
        in_specs=[pl.BlockSpec((2, dh, dh), lambda h: (0, 0, 0)),
                  pl.BlockSpec((1, dh, dh), lambda h: (h, 0, 0))],
        out_specs=pl.BlockSpec((1, dh, 2 * dh), lambda h: (h, 0, 0)),
        out_shape=jax.ShapeDtypeStruct((nh, dh, 2 * dh), bf16),
        compiler_params=_cparams("arbitrary"),
        name="fold_channel_dft",
    )(cs, w_f)


def _stream_specs(x, n_cols, n_ctx_tiles, rows=TM):
    if not isinstance(x, tuple):
        return [pl.BlockSpec((rows, n_cols), lambda i, *_: (i, 0))], [x]
    ctx = pl.BlockSpec((rows, n_cols), lambda i, *_: (jnp.minimum(i, n_ctx_tiles - 1), 0))
    lat = pl.BlockSpec((rows, n_cols), lambda i, *_: (jnp.maximum(i - n_ctx_tiles, 0), 0))
    return [ctx, lat], list(x)


def _stream_value(refs, n_ctx_tiles):
    if len(refs) == 1:
        return refs[0][...]
    return jnp.where(pl.program_id(0) < n_ctx_tiles, refs[0][...], refs[1][...])


def _stream_store(refs, y, n_ctx_tiles):
    if len(refs) == 1:
        refs[0][...] = y
        return
    is_ctx = pl.program_id(0) < n_ctx_tiles

    @pl.when(is_ctx)
    def _():
        refs[0][...] = y

    @pl.when(jnp.logical_not(is_ctx))
    def _():
        refs[1][...] = y


def _inproj_kernel(*refs, d_f, d_l, n_x, n_ctx_tiles):
    x = _stream_value(refs[:n_x], n_ctx_tiles)
    (sh_ref, sc_ref, g_ref, w_ref, ab_ref, pa_ref, pb_ref, ux_ref, ug_ref) = refs[n_x:]
    h = _rms(x, g_ref[...]) * (1.0 + sc_ref[0]) + sh_ref[0]
    u = jnp.dot(h.astype(bf16), w_ref[...], preferred_element_type=f32)
    nh = ab_ref.shape[0]
    dh = d_f // nh
    for hh in range(nh):
        sl = slice(hh * dh, (hh + 1) * dh)
        p = jnp.dot(u[:, sl].astype(bf16), ab_ref[hh], preferred_element_type=f32)
        pa_ref[:, sl] = p[:, :dh].astype(bf16)
        pb_ref[:, sl] = p[:, dh:].astype(bf16)
    ux_ref[...] = u[:, d_f:d_f + d_l]
    ug_ref[...] = u[:, d_f + d_l:]


def _in_proj(x, t, n_ctx_tiles, mods_l, g, w_in, ab, cond_of_tile, d_f, d_l):
    d, d_in = w_in.shape
    nh, dh, _ = ab.shape
    mod = lambda j: pl.BlockSpec((1, 1, d), lambda i: (cond_of_tile(i), 0, j))
    tok = lambda n: pl.BlockSpec((TM, n), lambda i: (i, 0))
    x_specs, x_args = _stream_specs(x, d, n_ctx_tiles)
    return pl.pallas_call(
        functools.partial(_inproj_kernel, d_f=d_f, d_l=d_l, n_x=len(x_args), n_ctx_tiles=n_ctx_tiles),
        grid=(t // TM,),
        in_specs=x_specs + [mod(0), mod(1), _const_spec((1, d)), _const_spec((d, d_in)),
                            _const_spec((nh, dh, 2 * dh))],
        out_specs=[tok(d_f), tok(d_f), tok(d_l), tok(d_l)],
        out_shape=[jax.ShapeDtypeStruct((t, d_f), bf16), jax.ShapeDtypeStruct((t, d_f), bf16),
                   jax.ShapeDtypeStruct((t, d_l), f32), jax.ShapeDtypeStruct((t, d_l), f32)],
        compiler_params=_cparams("arbitrary"),
        name="in_proj",
    )(*x_args, mods_l, mods_l, g, w_in, ab)


def _fourier_ctx_kernel(pa_ref, pb_ref, c_ref, ms_ref, o_ref):
    for b in range(pa_ref.shape[0]):
        y = jnp.dot(c_ref[...], pa_ref[b], preferred_element_type=f32)
        y = y + jnp.dot(ms_ref[...], pb_ref[b], preferred_element_type=f32)
        o_ref[b] = y.astype(bf16)


def _fourier_ctx(pa, pb, cpos, mspos, n_batch, seq):
    t, d_f = pa.shape
    nb = 4
    view = (t // seq, seq, d_f)
    blk = pl.BlockSpec((nb, seq, d_f), lambda g: (g, 0, 0))
    return pl.pallas_call(
        _fourier_ctx_kernel,
        grid=(n_batch // nb,),
        in_specs=[blk, blk, _const_spec((seq, seq)), _const_spec((seq, seq))],
        out_specs=blk,
        out_shape=jax.ShapeDtypeStruct((n_batch, seq, d_f), bf16),
        compiler_params=_cparams("arbitrary"),
        name="fourier_pos_ctx",
    )(pa.reshape(view), pb.reshape(view), cpos, mspos)


def _fourier_lat_kernel(pa_ref, pb_ref, c_ref, ms_ref, o_ref):
    y = jnp.dot(c_ref[...], pa_ref[0], preferred_element_type=f32)
    y = y + jnp.dot(ms_ref[...], pb_ref[0], preferred_element_type=f32)
    o_ref[0] = y.astype(bf16)


def _fourier_lat(pa, pb, c2, ms2, n_batch, seq, batch0):
    t, d_f = pa.shape
    view = (t // seq, seq, d_f)
    tq = min(FOURIER_TQ, seq)
    pblk = pl.BlockSpec((1, seq, d_f), lambda b, q: (batch0 + b, 0, 0))
    cblk = pl.BlockSpec((tq, seq), lambda b, q: (q, 0))
    return pl.pallas_call(
        _fourier_lat_kernel,
        grid=(n_batch, seq // tq),
        in_specs=[pblk, pblk, cblk, cblk],
        out_specs=pl.BlockSpec((1, tq, d_f), lambda b, q: (b, q, 0)),
        out_shape=jax.ShapeDtypeStruct((n_batch, seq, d_f), bf16),
        compiler_params=_cparams("arbitrary", "arbitrary"),
        name="fourier_pos_lat",
    )(pa.reshape(view), pb.reshape(view), c2, ms2)


def _gelu_tanh(x):
    return x * (0.5 * (1.0 + jnp.tanh(math.sqrt(2.0 / math.pi) * (x + 0.044715 * (x * x * x)))))


def _lru_kernel(*refs, direction, combine, nb, lc, ch, pitch, n_chunks):
    n_slab = ch // LANES
    (ux, hp, hn, cw, cb, wg, bg, lam, h0) = refs[:9]
    if combine:
        (hf, ug, y_out, st_out) = refs[9:13]
    else:
        (hf_out, st_out) = refs[9:11]
    a_scr = refs[-2 * n_slab - 1:-n_slab - 1]
    b_scr = refs[-n_slab - 1:-1]
    car = refs[-1]
    c = pl.program_id(2)
    cc = c if direction == 0 else n_chunks - 1 - c

    @pl.when(c == 0)
    def _():
        car[...] = h0[...]

    row = lax.broadcasted_iota(i32, (lc, ch), 0)
    z = -lam[...]
    softplus = jnp.maximum(z, 0.0) + jnp.log1p(jnp.exp(-jnp.abs(z)))
    for b in range(nb):
        x = ux[b]
        zero = jnp.zeros((1, ch), f32)
        p6 = jnp.where(cc > 0, hp[b, 6:7, :], zero)
        p7 = jnp.where(cc > 0, hp[b, 7:8, :], zero)
        n0 = jnp.where(cc < n_chunks - 1, hn[b, 0:1, :], zero)
        xm1 = jnp.where(row == 0, p7, pltpu.roll(x, 1, 0))
        xm2 = jnp.where(row == 0, p6, jnp.where(row == 1, p7, pltpu.roll(x, 2, 0)))
        xp1 = jnp.where(row == lc - 1, n0, pltpu.roll(x, lc - 1, 0))
        xc = cb[...] + xm2 * cw[0:1, :] + xm1 * cw[1:2, :] + x * cw[2:3, :] + xp1 * cw[3:4, :]
        gates = jnp.dot(xc.astype(bf16), wg[0], preferred_element_type=f32) + bg[0]
        r = _sigmoid(gates[:, :ch])
        i = _sigmoid(gates[:, ch:])
        log_a = (-RG_LRU_C * r) * softplus
        a = jnp.exp(log_a)
        th = jnp.tanh(log_a)
        mult = jnp.sqrt((-2.0 * th) / (1.0 - th))
        bx = mult * (i * xc)
        for s in range(n_slab):
            a_scr[s][b * pitch:b * pitch + lc, :] = a[:, s * LANES:(s + 1) * LANES]
            b_scr[s][b * pitch:b * pitch + lc, :] = bx[:, s * LANES:(s + 1) * LANES]

    def steps(k, hs):
        base = k * SCAN_UNROLL if direction == 0 else lc - (k + 1) * SCAN_UNROLL
        base = pl.multiple_of(base, SCAN_UNROLL)
        hs = list(hs)
        for j in range(SCAN_UNROLL):
            t = base + (j if direction == 0 else SCAN_UNROLL - 1 - j)
            for s in range(n_slab):
                idx = pl.ds(t, nb, stride=pitch)
                hs[s] = a_scr[s][idx, :] * hs[s] + b_scr[s][idx, :]
                b_scr[s][idx, :] = hs[s]
        return tuple(hs)

    hs = lax.fori_loop(0, lc // SCAN_UNROLL, steps,
                       tuple(car[:, s * LANES:(s + 1) * LANES] for s in range(n_slab)))
    for s in range(n_slab):
        car[:, s * LANES:(s + 1) * LANES] = hs[s]
    st_out[...] = car[...]

    for b in range(nb):
        for s in range(n_slab):
            sl = slice(s * LANES, (s + 1) * LANES)
            h = b_scr[s][b * pitch:b * pitch + lc, :]
            if combine:
                y_out[b, :, sl] = ((hf[b, :, sl] + h) * _gelu_tanh(ug[b, :, sl])).astype(bf16)
            else:
                hf_out[b, :, sl] = h


def _lru_pass(direction, ux, ug, hf, conv_w, conv_b, wg, bg, lam, h0, n_batch, seq, batch0, nb):
    t, d_l = ux.shape
    ch = d_l // N_HEADS
    lc = min(LRU_CHUNK, seq)
    n_chunks = seq // lc
    pitch = lc + SUBLANES
    view = (t // seq, seq, d_l)
    assert batch0 % nb == 0 and n_batch % nb == 0
    g0 = batch0 // nb
    combine = direction == 1
    pos = (lambda c: c) if direction == 0 else (lambda c: n_chunks - 1 - c)
    l8 = lc // SUBLANES
    n8 = seq // SUBLANES
    main = pl.BlockSpec((nb, lc, ch), lambda g, h, c: (g0 + g, pos(c), h))
    prev = pl.BlockSpec((nb, SUBLANES, ch), lambda g, h, c: (g0 + g, jnp.maximum(pos(c) * l8 - 1, 0), h))
    nxt = pl.BlockSpec((nb, SUBLANES, ch), lambda g, h, c: (g0 + g, jnp.minimum((pos(c) + 1) * l8, n8 - 1), h))
    local = pl.BlockSpec((nb, lc, ch), lambda g, h, c: (g, pos(c), h))
    head_row = lambda rows: pl.BlockSpec((rows, ch), lambda g, h, c: (0, h))
    state = pl.BlockSpec((nb, ch), lambda g, h, c: (g, h))
    in_specs = [main, prev, nxt, head_row(CONV_W), head_row(1),
                pl.BlockSpec((1, ch, 2 * ch), lambda g, h, c: (h, 0, 0)),
                pl.BlockSpec((1, 1, 2 * ch), lambda g, h, c: (h, 0, 0)),
                head_row(1), state]
    ux3 = ux.reshape(view)
    args = [ux3, ux3, ux3, conv_w, conv_b, wg, bg, lam, h0]
    scratch = [pltpu.VMEM((nb * pitch, LANES), f32) for _ in range(2 * (ch // LANES))]
    scratch.append(pltpu.VMEM((nb, ch), f32))
    st_shape = jax.ShapeDtypeStruct((n_batch, d_l), f32)
    if combine:
        in_specs += [local, main]
        args += [hf, ug.reshape(view)]
    out_specs = [local, state]
    out_shape = [jax.ShapeDtypeStruct((n_batch, seq, d_l), bf16 if combine else f32), st_shape]
    kern = functools.partial(_lru_kernel, direction=direction, combine=combine,
                             nb=nb, lc=lc, ch=ch, pitch=pitch, n_chunks=n_chunks)
    return pl.pallas_call(
        kern,
        grid=(n_batch // nb, N_HEADS, n_chunks),
        in_specs=in_specs, out_specs=out_specs, out_shape=out_shape,
        scratch_shapes=scratch,
        compiler_params=_cparams("arbitrary", "arbitrary", "arbitrary"),
        name="rglru_dir%d" % direction,
    )(*args)


def _outproj_kernel(*refs, d_f, n_ctx_tiles, n_x, routed):
    yf = _stream_value(refs[0:2], n_ctx_tiles)
    yl = _stream_value(refs[2:4], n_ctx_tiles)
    x = _stream_value(refs[4:4 + n_x], n_ctx_tiles)
    (g1, sh2, sc2, gnf, gnl, n2g, wo) = refs[4 + n_x:11 + n_x]
    if routed:
        (rw, rb, x1_out, hm_out, lg_out) = refs[11 + n_x:]
    else:
        (x1_out, hm_out) = refs[11 + n_x:]
    yfn = _rms(yf.astype(f32), gnf[...]).astype(bf16)
    yln = _rms(yl.astype(f32), gnl[...]).astype(bf16)
    out = jnp.dot(yfn, wo[:d_f, :], preferred_element_type=f32)
    out = out + jnp.dot(yln, wo[d_f:, :], preferred_element_type=f32)
    x1 = x + g1[0] * out
    x1_out[...] = x1
    hm = _rms(x1, n2g[...]) * (1.0 + sc2[0]) + sh2[0]
    hm_out[...] = hm.astype(hm_out.dtype)
    if routed:
        ne = rw.shape[1] // 2
        hi = hm.astype(bf16)
        lo = (hm - hi.astype(f32)).astype(bf16)
        big = jnp.dot(hi, rw[...], preferred_element_type=f32)
        small = jnp.dot(lo, rw[:, :ne], preferred_element_type=f32)
        lg_out[...] = big[:, :ne] + (big[:, ne:] + small) + rb[...]


def _out_proj(yf, yl, x, t, mods_l, gnf, gnl, n2g, w_out, cond_of_tile, router):
    d = w_out.shape[1]
    d_f = yf[0].shape[1]
    d_l = yl[0].shape[1]
    nct = yf[0].shape[0] // TM
    routed = router is not None
    mod = lambda j: pl.BlockSpec((1, 1, d), lambda i: (cond_of_tile(i), 0, j))
    tok = lambda n: pl.BlockSpec((TM, n), lambda i: (i, 0))
    yf_specs, yf_args = _stream_specs(yf, d_f, nct)
    yl_specs, yl_args = _stream_specs(yl, d_l, nct)
    x_specs, x_args = _stream_specs(x, d, nct)
    in_specs = yf_specs + yl_specs + x_specs + [
        mod(2), mod(3), mod(4), _const_spec((1, d_f)), _const_spec((1, d_l)), _const_spec((1, d)),
        _const_spec((d_f + d_l, d))]
    args = yf_args + yl_args + x_args + [mods_l, mods_l, mods_l, gnf, gnl, n2g, w_out]
    out_specs = [tok(d), tok(d)]
    out_shape = [jax.ShapeDtypeStruct((t, d), f32), jax.ShapeDtypeStruct((t, d), f32 if routed else bf16)]
    if routed:
        rw, rb = router
        in_specs += [_const_spec(rw.shape), _const_spec(rb.shape)]
        args += [rw, rb]
        out_specs.append(tok(LANES))
        out_shape.append(jax.ShapeDtypeStruct((t, LANES), f32))
    return pl.pallas_call(
        functools.partial(_outproj_kernel, d_f=d_f, n_ctx_tiles=nct, n_x=len(x_args), routed=routed),
        grid=(t // TM,),
        in_specs=in_specs, out_specs=out_specs, out_shape=out_shape,
        compiler_params=_cparams("arbitrary"),
        name="out_proj",
    )(*args)


def _ffn_kernel(*refs, final, n_ctx_tiles):
    if final:
        hm, x1, g2, wg, wu, wd, fg = refs[:7]
        o_refs = refs[7:9]
    else:
        hm, x1, g2, wg, wu, wd = refs[:6]
        o_refs = refs[6:7]
    acc = refs[-1]
    f = pl.program_id(1)

    @pl.when(f == 0)
    def _():
        acc[...] = jnp.zeros(acc.shape, f32)

    xb = hm[...]
    g = jnp.dot(xb, wg[...], preferred_element_type=f32)
    u = jnp.dot(xb, wu[...], preferred_element_type=f32)
    h = ((g * _sigmoid(g)) * u).astype(bf16)
    acc[...] += jnp.dot(h, wd[...], preferred_element_type=f32)

    @pl.when(f == pl.num_programs(1) - 1)
    def _():
        y = x1[...] + g2[0] * acc[...]
        _stream_store(o_refs, _rms(y, fg[...]) if final else y, n_ctx_tiles)


def _final_out(t, t_ctx, d, rows, final):
    shapes = (jax.ShapeDtypeStruct((t_ctx, d), f32), jax.ShapeDtypeStruct((t - t_ctx, d), f32))
    specs, _ = _stream_specs(shapes if final else None, d, t_ctx // rows, rows)
    return specs, (list(shapes) if final else [jax.ShapeDtypeStruct((t, d), f32)])


def _ffn_dense(hm, x1, mods_l, wg, wu, wd, m, cond_of_tile, final_g, t_ctx):
    t, d = x1.shape
    dff = wg.shape[2]
    final = final_g is not None
    tok = pl.BlockSpec((TMD, d), lambda i, f: (i, 0))
    in_specs = [tok, tok, pl.BlockSpec((1, 1, d), lambda i, f: (cond_of_tile(i), 0, 5)),
                pl.BlockSpec((None, d, TFD), lambda i, f: (m, 0, f)),
                pl.BlockSpec((None, d, TFD), lambda i, f: (m, 0, f)),
                pl.BlockSpec((None, TFD, d), lambda i, f: (m, f, 0))]
    args = [hm, x1, mods_l, wg, wu, wd]
    if final:
        in_specs.append(pl.BlockSpec((1, d), lambda i, f: (0, 0)))
        args.append(final_g)
    out_specs, out_shape = _final_out(t, t_ctx, d, TMD, final)
    out = pl.pallas_call(
        functools.partial(_ffn_kernel, final=final, n_ctx_tiles=t_ctx // TMD),
        grid=(t // TMD, dff // TFD),
        in_specs=in_specs, out_specs=out_specs, out_shape=out_shape,
        scratch_shapes=[pltpu.VMEM((TMD, d), f32)],
        compiler_params=_cparams("arbitrary", "arbitrary"),
        name="ffn_dense",
    )(*args)
    return tuple(out) if final else out[0]


def _route_kernel(lg_ref, meta_ref, cnt_ref, run, *, n_exp):
    i = pl.program_id(0)
    tr = lg_ref.shape[0]

    @pl.when(i == 0)
    def _():
        run[...] = jnp.zeros(run.shape, f32)

    lane = lax.broadcasted_iota(i32, (tr, LANES), 1)
    lane_f = lane.astype(f32)
    neg = jnp.float32(-jnp.inf)
    lg = jnp.where(lane < n_exp, lg_ref[...], neg)
    m1 = jnp.max(lg, axis=1, keepdims=True)
    i1 = jnp.min(jnp.where(lg == m1, lane_f, float(LANES)), axis=1, keepdims=True)
    oh1 = lane_f == i1
    lg2 = jnp.where(oh1, neg, lg)
    m2 = jnp.max(lg2, axis=1, keepdims=True)
    i2 = jnp.min(jnp.where(lg2 == m2, lane_f, float(LANES)), axis=1, keepdims=True)
    oh2 = lane_f == i2
    e = jnp.exp(m2 - m1)
    g1 = 1.0 / (1.0 + e)
    g2 = e / (1.0 + e)
    sel = jnp.where(oh1 | oh2, 1.0, 0.0)
    rr = lax.broadcasted_iota(i32, (tr, tr), 0)
    rc = lax.broadcasted_iota(i32, (tr, tr), 1)
    tri = jnp.where(rc < rr, 1.0, 0.0).astype(bf16)
    pos = jnp.dot(tri, sel.astype(bf16), preferred_element_type=f32) + run[...]
    p1 = jnp.sum(jnp.where(oh1, pos, 0.0), axis=1, keepdims=True)
    p2 = jnp.sum(jnp.where(oh2, pos, 0.0), axis=1, keepdims=True)
    run[...] = run[...] + jnp.sum(sel, axis=0, keepdims=True)
    cnt_ref[...] = run[...]
    cols = (i1, i2, p1, p2, g1, g2)
    meta = jnp.zeros((tr, LANES), f32)
    for k, v in enumerate(cols):
        meta = jnp.where(lane == k, v, meta)
    meta_ref[...] = meta


def _route(logits, n_exp):
    t = logits.shape[0]
    return pl.pallas_call(
        functools.partial(_route_kernel, n_exp=n_exp),
        grid=(t // TM,),
        in_specs=[pl.BlockSpec((TM, LANES), lambda i: (i, 0))],
        out_specs=[pl.BlockSpec((TM, LANES), lambda i: (i, 0)), pl.BlockSpec((1, LANES), lambda i: (0, 0))],
        out_shape=[jax.ShapeDtypeStruct((t, LANES), f32), jax.ShapeDtypeStruct((1, LANES), f32)],
        scratch_shapes=[pltpu.VMEM((1, LANES), f32)],
        compiler_params=_cparams("arbitrary"),
        name="route_top2",
    )(logits)


def _row_copy(src, src_row, dst, dst_row, sem):
    return pltpu.make_async_copy(src.at[pl.ds(src_row, 1), :], dst.at[pl.ds(dst_row, 1), :], sem)


def _scatter_kernel(d1, d2, hm_ref, xs_in, xs_out, sem):
    del xs_in
    i = pl.program_id(0)

    def issue(r, c):
        t = i * TROW + r
        _row_copy(hm_ref, r, xs_out, d1[t], sem.at[0]).start(priority=0)
        _row_copy(hm_ref, r, xs_out, d2[t], sem.at[1]).start(priority=1)
        return c

    def drain(r, c):
        _row_copy(hm_ref, 0, xs_out, 0, sem.at[0]).wait()
        _row_copy(hm_ref, 0, xs_out, 0, sem.at[1]).wait()
        return c

    lax.fori_loop(0, TROW, issue, 0, unroll=ROW_UNROLL)
    lax.fori_loop(0, TROW, drain, 0, unroll=ROW_UNROLL)


def _scatter_rows(d1, d2, hm, xs_zero):
    t, d = hm.shape
    return pl.pallas_call(
        _scatter_kernel,
        grid_spec=pltpu.PrefetchScalarGridSpec(
            num_scalar_prefetch=2, grid=(t // TROW,),
            in_specs=[pl.BlockSpec((TROW, d), lambda i, a, b: (i, 0)), pl.BlockSpec(memory_space=pl.ANY)],
            out_specs=pl.BlockSpec(memory_space=pl.ANY),
            scratch_shapes=[pltpu.SemaphoreType.DMA((2,))]),
        out_shape=jax.ShapeDtypeStruct(xs_zero.shape, f32),
        input_output_aliases={3: 0},
        compiler_params=_cparams("arbitrary"),
        name="moe_scatter_rows",
    )(d1, d2, hm, xs_zero)


def _moe_kernel(te, nu, nv, xs, wg, wu, wd, o_ref, xb):
    del te, nu
    j = pl.program_id(0)
    f = pl.program_id(1)
    rows = nv[j]

    @pl.when(f == 0)
    def _():
        o_ref[...] = jnp.zeros(o_ref.shape, f32)

    @pl.when((f == 0) & (rows > 0))
    def _():
        xb[...] = xs[...].astype(bf16)

    def swiglu_rows(n_rows):
        x = xb[:n_rows, :]
        g = jnp.dot(x, wg[0].astype(bf16), preferred_element_type=f32)
        u = jnp.dot(x, wu[0].astype(bf16), preferred_element_type=f32)
        h = ((g * _sigmoid(g)) * u).astype(bf16)
        o_ref[:n_rows, :] += jnp.dot(h, wd[0].astype(bf16), preferred_element_type=f32)

    n_piece = xb.shape[0] // MOE_SUB
    for p in range(1, n_piece + 1):
        @pl.when((rows > (p - 1) * MOE_SUB) & (rows <= p * MOE_SUB))
        def _(p=p):
            swiglu_rows(p * MOE_SUB)


def _moe_grouped(tile_expert, n_used, tile_rows, xs, wg, wu, wd, m):
    r, d = xs.shape
    dff = wg.shape[3]
    nf = dff // TFM
    row = lambda j, f, te, nu, nv: (jnp.minimum(j, nu[0] - 1), 0)
    fidx = lambda j, f, nu: jnp.where(j < nu[0], f, nf - 1)
    return pl.pallas_call(
        _moe_kernel,
        grid_spec=pltpu.PrefetchScalarGridSpec(
            num_scalar_prefetch=3, grid=(r // TMM, nf),
            in_specs=[pl.BlockSpec((TMM, d), row, pipeline_mode=pl.Buffered(MOE_XS_BUFFERS)),
                      pl.BlockSpec((None, 1, d, TFM), lambda j, f, te, nu, nv: (m, te[j], 0, fidx(j, f, nu))),
                      pl.BlockSpec((None, 1, d, TFM), lambda j, f, te, nu, nv: (m, te[j], 0, fidx(j, f, nu))),
                      pl.BlockSpec((None, 1, TFM, d), lambda j, f, te, nu, nv: (m, te[j], fidx(j, f, nu), 0))],
            out_specs=pl.BlockSpec((TMM, d), lambda j, f, te, nu, nv: (j, 0)),
            scratch_shapes=[pltpu.VMEM((TMM, d), bf16)]),
        out_shape=jax.ShapeDtypeStruct((r, d), f32),
        compiler_params=pltpu.CompilerParams(dimension_semantics=("arbitrary", "arbitrary"),
                                             vmem_limit_bytes=MOE_VMEM_LIMIT_BYTES),
        name="moe_grouped_swiglu",
    )(tile_expert, n_used, tile_rows, xs, wg, wu, wd)


def _combine_kernel(*refs, final, n_ctx_tiles):
    if final:
        d1, d2, x1, g2, meta, fg, ys = refs[:7]
        o_refs = refs[7:9]
    else:
        d1, d2, x1, g2, meta, ys = refs[:6]
        o_refs = refs[6:7]
    buf, sem = refs[-2:]
    i = pl.program_id(0)
    slot = i % 2

    def issue(step, s):
        def body(r, c):
            t = step * TROW + r
            _row_copy(ys, d1[t], buf.at[s, 0], r, sem.at[s, 0]).start(priority=0)
            _row_copy(ys, d2[t], buf.at[s, 1], r, sem.at[s, 1]).start(priority=1)
            return c
        lax.fori_loop(0, TROW, body, 0, unroll=ROW_UNROLL)

    def drain(r, c):
        _row_copy(ys, 0, buf.at[slot, 0], 0, sem.at[slot, 0]).wait()
        _row_copy(ys, 0, buf.at[slot, 1], 0, sem.at[slot, 1]).wait()
        return c

    @pl.when(i == 0)
    def _():
        issue(0, 0)

    @pl.when(i + 1 < pl.num_programs(0))
    def _():
        issue(i + 1, 1 - slot)

    lax.fori_loop(0, TROW, drain, 0, unroll=ROW_UNROLL)
    m = meta[...]
    ff = m[:, 4:5] * buf[slot, 0] + m[:, 5:6] * buf[slot, 1]
    y = x1[...] + g2[0] * ff
    _stream_store(o_refs, _rms(y, fg[...]) if final else y, n_ctx_tiles)


def _combine_rows(d1, d2, x1, mods_l, meta, ys, cond_of_tile_row, final_g, t_ctx):
    t, d = x1.shape
    final = final_g is not None
    tok = lambda n: pl.BlockSpec((TROW, n), lambda i, a, b: (i, 0))
    in_specs = [tok(d), pl.BlockSpec((1, 1, d), lambda i, a, b: (cond_of_tile_row(i), 0, 5)), tok(LANES)]
    args = [x1, mods_l, meta]
    if final:
        in_specs.append(pl.BlockSpec((1, d), lambda i, a, b: (0, 0)))
        args.append(final_g)
    in_specs.append(pl.BlockSpec(memory_space=pl.ANY))
    args.append(ys)
    out_specs, out_shape = _final_out(t, t_ctx, d, TROW, final)
    out = pl.pallas_call(
        functools.partial(_combine_kernel, final=final, n_ctx_tiles=t_ctx // TROW),
        grid_spec=pltpu.PrefetchScalarGridSpec(
            num_scalar_prefetch=2, grid=(t // TROW,),
            in_specs=in_specs, out_specs=out_specs,
            scratch_shapes=[pltpu.VMEM((2, TOP_K, TROW, d), f32), pltpu.SemaphoreType.DMA((2, TOP_K))]),
        out_shape=out_shape,
        compiler_params=_cparams("arbitrary"),
        name="moe_combine_rows",
    )(d1, d2, *args)
    return tuple(out) if final else out[0]


def _moe_ffn(hm, logits, x1, mods_l, wg, wu, wd, m, cond_of_tile_row, final_g, t_ctx):
    t, d = hm.shape
    n_exp = wg.shape[1]
    assert (TOP_K * t) % TMM == 0 and TMM % MOE_SUB == 0
    meta, counts = _route(logits, n_exp)
    cnt = counts[0, :n_exp].astype(i32)
    padded = ((cnt + TMM - 1) // TMM) * TMM
    ends = jnp.cumsum(padded)
    off = ends - padded
    e1 = meta[:, 0].astype(i32)
    e2 = meta[:, 1].astype(i32)
    d1 = off[e1] + meta[:, 2].astype(i32)
    d2 = off[e2] + meta[:, 3].astype(i32)
    n_tiles = (TOP_K * t) // TMM + n_exp
    tile_start = jnp.arange(n_tiles, dtype=i32) * TMM
    n_used = (ends[-1] // TMM).reshape(1).astype(i32)
    tile_expert = jnp.minimum(jnp.sum((tile_start[:, None] >= ends[None, :]).astype(i32), axis=1), n_exp - 1)
    tile_rows = jnp.clip((off + cnt)[tile_expert] - tile_start, 0, TMM)
    tile_rows = jnp.where(tile_start < ends[-1], tile_rows, 0)
    tile_expert = jnp.where(tile_start < ends[-1], tile_expert, tile_expert[n_used[0] - 1])
    xs = _scatter_rows(d1, d2, hm, jnp.zeros((n_tiles * TMM, d), f32))
    ys = _moe_grouped(tile_expert, n_used, tile_rows, xs, wg, wu, wd, m)
    return _combine_rows(d1, d2, x1, mods_l, meta, ys, cond_of_tile_row, final_g, t_ctx)


@functools.lru_cache(maxsize=None)
def _dft_tables(n_ctx, rows, grid_w, dh):
    def cos_sin(k, period):
        ang = 2.0 * np.pi * (k % period).astype(np.float64) / period
        return np.cos(ang), np.sin(ang)

    k = np.arange(dh)
    cc, sc = cos_sin(k[:, None] * k[None, :], dh)
    chan = np.stack([cc, sc]).astype(np.float32)
    t = np.arange(n_ctx)
    cp, sp = cos_sin(t[:, None] * t[None, :], n_ctx)
    s_ctx = 1.0 / math.sqrt(n_ctx * dh)
    n = np.arange(rows * grid_w)
    r, c = n // grid_w, n % grid_w
    period = rows * grid_w // math.gcd(rows, grid_w)
    phase = (r[:, None] * r[None, :]) * (period // rows) + (c[:, None] * c[None, :]) * (period // grid_w)
    c2, s2 = cos_sin(phase, period)
    s_lat = 1.0 / math.sqrt(rows * grid_w * dh)
    as_f32 = lambda a: np.asarray(a, np.float32)
    return chan, as_f32(cp * s_ctx), as_f32(-sp * s_ctx), as_f32(c2 * s_lat), as_f32(-s2 * s_lat)


def kernel(x_prompt, x_sample, state_lru, c, c_ctx, norm1_g, norm2_g, w_mod, b_mod, w_in, w_fourier,
           gn_fourier_g, conv_w, conv_b, w_r, b_r, w_i, b_i, lam, gn_lru_g, w_out, ffn_w_gate, ffn_w_up,
           ffn_w_down, router_w, router_b, moe_w_gate, moe_w_up, moe_w_down, final_g):
    nb_ctx, seq_ctx, d = x_prompt.shape
    nb_lat, seq_lat, _ = x_sample.shape
    depth = w_mod.shape[0]
    d_f = w_fourier.shape[1] * w_fourier.shape[2]
    d_l = lam.shape[2]
    dh = d_f // N_HEADS
    t_ctx = nb_ctx * seq_ctx
    t_lat = nb_lat * seq_lat
    assert t_ctx % TMD == 0 and seq_lat % TMD == 0 and TMD % TM == 0
    assert t_ctx % seq_lat == 0 and seq_lat % GRID_W == 0
    assert d_l // N_HEADS == dh and dh % LANES == 0 and TM % TROW == 0

    def cond_of(tile_rows):
        return lambda i: jnp.where(i * tile_rows < t_ctx, 0,
                                   1 + jnp.maximum(i * tile_rows - t_ctx, 0) // seq_lat)

    cond_tm, cond_row = cond_of(TM), cond_of(TROW)

    n_cond = 1 + nb_lat
    ncp = -(-n_cond // SUBLANES) * SUBLANES
    cond = jnp.concatenate([c_ctx[None, :], c, jnp.zeros((ncp - n_cond, d), f32)], axis=0)
    mods = _mods(cond.T, w_mod, b_mod, n_cond).reshape(depth, ncp, 1, N_MOD * d)

    chan, cp, msp, c2, ms2 = _dft_tables(seq_ctx, seq_lat // GRID_W, GRID_W, dh)
    chan = jnp.asarray(chan)
    cp, msp, c2, ms2 = (jnp.asarray(a).astype(bf16) for a in (cp, msp, c2, ms2))

    t = t_ctx + t_lat
    x = (x_prompt.reshape(t_ctx, d), x_sample.reshape(t_lat, d))
    zeros_h0 = jnp.zeros((nb_ctx, d_l), f32)
    ctx_states = []
    for l in range(depth):
        mods_l = mods[l]
        ab = _fold_channel_dft(chan, w_fourier[l])
        pa, pb, ux, ug = _in_proj(x, t, t_ctx // TM, mods_l, norm1_g[l][None, :], w_in[l].astype(bf16), ab,
                                  cond_tm, d_f, d_l)

        yf_c = _fourier_ctx(pa, pb, cp, msp, nb_ctx, seq_ctx).reshape(t_ctx, d_f)
        yf_l = _fourier_lat(pa, pb, c2, ms2, nb_lat, seq_lat, t_ctx // seq_lat).reshape(t_lat, d_f)

        def gate_w(dr):
            wgt = jnp.concatenate([w_r[l, dr], w_i[l, dr]], axis=-1).astype(bf16)
            bias = jnp.concatenate([b_r[l, dr].reshape(N_HEADS, 1, dh), b_i[l, dr].reshape(N_HEADS, 1, dh)], axis=-1)
            return wgt, bias

        yls = []
        states = []
        for (n_batch, seq, batch0, nb, h0) in (
                (nb_ctx, seq_ctx, 0, SUBLANES, (zeros_h0, zeros_h0)),
                (nb_lat, seq_lat, t_ctx // seq_lat, nb_lat,
                 (state_lru[:, l, 0], state_lru[:, l, 1]))):
            wg0, bg0 = gate_w(0)
            hf, st_f = _lru_pass(0, ux, None, None, conv_w[l], conv_b[l][None, :], wg0, bg0,
                                 lam[l, 0][None, :], h0[0], n_batch, seq, batch0, nb)
            wg1, bg1 = gate_w(1)
            yl, st_b = _lru_pass(1, ux, ug, hf, conv_w[l], conv_b[l][None, :], wg1, bg1,
                                 lam[l, 1][None, :], h0[1], n_batch, seq, batch0, nb)
            yls.append(yl.reshape(n_batch * seq, d_l))
            states.append((st_f, st_b))
        ctx_states.append(jnp.stack(states[0], axis=1))
        mixed = ((yf_c, yf_l), (yls[0], yls[1]))

        m = l // 2
        last = l == depth - 1
        fg = final_g[None, :] if last else None
        if l % 2 == 0:
            x1, hm = _out_proj(*mixed, x, t, mods_l, gn_fourier_g[l][None, :], gn_lru_g[l][None, :],
                               norm2_g[l][None, :], w_out[l].astype(bf16), cond_tm, None)
            x = _ffn_dense(hm, x1, mods_l, ffn_w_gate.astype(bf16), ffn_w_up.astype(bf16),
                           ffn_w_down.astype(bf16), m, cond_of(TMD), fg, t_ctx)
        else:
            n_exp = router_w.shape[2]
            rw = jnp.zeros((d, LANES), f32).at[:, :n_exp].set(router_w[m])
            rw_hi = rw.astype(bf16)
            rw_lo = (rw - rw_hi.astype(f32)).astype(bf16)
            rb = jnp.zeros((1, LANES), f32).at[0, :n_exp].set(router_b[m])
            x1, hm, logits = _out_proj(*mixed, x, t, mods_l, gn_fourier_g[l][None, :], gn_lru_g[l][None, :],
                                       norm2_g[l][None, :], w_out[l].astype(bf16), cond_tm,
                                       (jnp.concatenate([rw_hi, rw_lo], axis=1), rb))
            x = _moe_ffn(hm, logits, x1, mods_l, moe_w_gate, moe_w_up, moe_w_down, m, cond_row, fg, t_ctx)
    y_prompt = x[0].reshape(nb_ctx, seq_ctx, d)
    y_sample = x[1].reshape(nb_lat, seq_lat, d)
    new_state = jnp.stack(ctx_states, axis=1).astype(state_lru.dtype)
    return (y_prompt, y_sample, new_state)
```

```python
import functools
import math

import jax
import jax.numpy as jnp
import numpy as np
from jax import lax
from jax.experimental import pallas as pl
from jax.experimental.pallas import tpu as pltpu

f32 = jnp.float32
bf16 = jnp.bfloat16
i32 = jnp.int32

GRID_W = 64
N_HEADS = 4
CONV_W = 4
RG_LRU_C = 8.0
N_MOD = 6
TOP_K = 2
EPS = 1e-6

LANES = 128
SUBLANES = 8
VMEM_LIMIT_BYTES = 56 * 1024 * 1024

MOE_VMEM_LIMIT_BYTES = 60 * 1024 * 1024

TM = 512
TMD = 1024
TMD_FINAL = 512
TFD = 512
TMM = 1024
MOE_SUB = 512
TFM = 256
MOE_XS_BUFFERS = 2
LRU_CHUNK = 256
SCAN_UNROLL = 8
TROW = 256
ROW_UNROLL = 8
FOURIER_TQ = 512


def _cparams(*sem):
    return pltpu.CompilerParams(dimension_semantics=sem, vmem_limit_bytes=VMEM_LIMIT_BYTES)


def _rms(x, g):
    return x * lax.rsqrt(jnp.mean(x * x, axis=-1, keepdims=True) + EPS) * g


def _sigmoid(x):
    return 1.0 / (1.0 + jnp.exp(-x))


def _const_spec(shape):
    nd = len(shape)
    return pl.BlockSpec(shape, lambda *_: (0,) * nd, pipeline_mode=pl.Buffered(1))


def _mod_kernel(ct_ref, w_ref, b_ref, o_ref, sb, *, n_cond, kc):
    d = ct_ref.shape[0]
    n_lt = w_ref.shape[2] // LANES

    @pl.when((pl.program_id(0) == 0) & (pl.program_id(1) == 0))
    def _():
        c = ct_ref[...]
        s = c * _sigmoid(c)
        for r in range(n_cond):
            sb[r] = jnp.broadcast_to(s[:, r:r + 1], (d, LANES))

    def body(k, accs):
        k0 = pl.multiple_of(k * kc, kc)
        ws = [w_ref[0, pl.ds(k0, kc), lt * LANES:(lt + 1) * LANES] for lt in range(n_lt)]
        out = []
        for r in range(n_cond):
            sr = sb[r, pl.ds(k0, kc), :]
            for lt in range(n_lt):
                part = (ws[lt] * sr).reshape(kc // SUBLANES, SUBLANES, LANES).sum(axis=0)
                out.append(accs[r * n_lt + lt] + part)
        return tuple(out)

    zero = jnp.zeros((SUBLANES, LANES), f32)
    accs = lax.fori_loop(0, d // kc, body, tuple(zero for _ in range(n_cond * n_lt)), unroll=2)
    o_ref[...] = jnp.zeros(o_ref.shape, f32)
    for r in range(n_cond):
        for lt in range(n_lt):
            sl = slice(lt * LANES, (lt + 1) * LANES)
            o_ref[0, r:r + 1, sl] = jnp.sum(accs[r * n_lt + lt], axis=0, keepdims=True) + b_ref[0, :, sl]


def _mods(cond_t, w_mod, b_mod, n_cond):
    depth, d, n6 = w_mod.shape
    ncp = cond_t.shape[1]
    tn = 768
    return pl.pallas_call(
        functools.partial(_mod_kernel, n_cond=n_cond, kc=16),
        grid=(depth, n6 // tn),
        in_specs=[pl.BlockSpec((d, ncp), lambda l, j: (0, 0)),
                  pl.BlockSpec((1, d, tn), lambda l, j: (l, 0, j)),
                  pl.BlockSpec((1, 1, tn), lambda l, j: (l, 0, j))],
        out_specs=pl.BlockSpec((1, ncp, tn), lambda l, j: (l, 0, j)),
        out_shape=jax.ShapeDtypeStruct((depth, ncp, n6), f32),
        scratch_shapes=[pltpu.VMEM((n_cond, d, LANES), f32)],
        compiler_params=_cparams("arbitrary", "arbitrary"),
        name="adaln_mods",
    )(cond_t, w_mod, b_mod.reshape(depth, 1, n6))


def _ab_kernel(cs_ref, wf_ref, o_ref):
    dh = wf_ref.shape[1]
    w = wf_ref[0]
    o_ref[0, :, :dh] = jnp.dot(cs_ref[0], w, precision=lax.Precision.HIGHEST,
                               preferred_element_type=f32).astype(bf16)
    o_ref[0, :, dh:] = jnp.dot(cs_ref[1], w, precision=lax.Precision.HIGHEST,
                               preferred_element_type=f32).astype(bf16)


def _fold_channel_dft(cs, w_f):
    nh, dh, _ = w_f.shape
    return pl.pallas_call(
        _ab_kernel,
        grid=(nh,),
        in_specs=[pl.BlockSpec((2, dh, dh), lambda h: (0, 0, 0)),
                  pl.BlockSpec((1, dh, dh), lambda h: (h, 0, 0))],
        out_specs=pl.BlockSpec((1, dh, 2 * dh), lambda h: (h, 0, 0)),
        out_shape=jax.ShapeDtypeStruct((nh, dh, 2 * dh), bf16),
        compiler_params=_cparams("arbitrary"),
        name="fold_channel_dft",
    )(cs, w_f)


def _stream_specs(x, n_cols, n_ctx_tiles, rows=TM):
    if not isinstance(x, tuple):
        return [pl.BlockSpec((rows, n_cols), lambda i, *_: (i, 0))], [x]
    ctx = pl.BlockSpec((rows, n_cols), lambda i, *_: (jnp.minimum(i, n_ctx_tiles - 1), 0))
    lat = pl.BlockSpec((rows, n_cols), lambda i, *_: (jnp.maximum(i - n_ctx_tiles, 0), 0))
    return [ctx, lat], list(x)


def _stream_value(refs, n_ctx_tiles):
    if len(refs) == 1:
        return refs[0][...]
    return jnp.where(pl.program_id(0) < n_ctx_tiles, refs[0][...], refs[1][...])


def _stream_store(refs, y, n_ctx_tiles):
    if len(refs) == 1:
        refs[0][...] = y
        return
    is_ctx = pl.program_id(0) < n_ctx_tiles

    @pl.when(is_ctx)
    def _():
        refs[0][...] = y

    @pl.when(jnp.logical_not(is_ctx))
    def _():
        refs[1][...] = y


def _inproj_kernel(*refs, d_f, d_l, n_x, n_ctx_tiles):
    x = _stream_value(refs[:n_x], n_ctx_tiles)
    (sh_ref, sc_ref, g_ref, w_ref, ab_ref, pa_ref, pb_ref, ux_ref, ug_ref) = refs[n_x:]
    h = _rms(x, g_ref[...]) * (1.0 + sc_ref[0]) + sh_ref[0]
    u = jnp.dot(h.astype(bf16), w_ref[...], preferred_element_type=f32)
    nh = ab_ref.shape[0]
    dh = d_f // nh
    for hh in range(nh):
        sl = slice(hh * dh, (hh + 1) * dh)
        p = jnp.dot(u[:, sl].astype(bf16), ab_ref[hh], preferred_element_type=f32)
        pa_ref[:, sl] = p[:, :dh].astype(bf16)
        pb_ref[:, sl] = p[:, dh:].astype(bf16)
    ux_ref[...] = u[:, d_f:d_f + d_l]
    ug_ref[...] = u[:, d_f + d_l:]


def _in_proj(x, t, n_ctx_tiles, mods_l, g, w_in, ab, cond_of_tile, d_f, d_l):
    d, d_in = w_in.shape
    nh, dh, _ = ab.shape
    mod = lambda j: pl.BlockSpec((1, 1, d), lambda i: (cond_of_tile(i), 0, j))
    tok = lambda n: pl.BlockSpec((TM, n), lambda i: (i, 0))
    x_specs, x_args = _stream_specs(x, d, n_ctx_tiles)
    return pl.pallas_call(
        functools.partial(_inproj_kernel, d_f=d_f, d_l=d_l, n_x=len(x_args), n_ctx_tiles=n_ctx_tiles),
        grid=(t // TM,),
        in_specs=x_specs + [mod(0), mod(1), _const_spec((1, d)), _const_spec((d, d_in)),
                            _const_spec((nh, dh, 2 * dh))],
        out_specs=[tok(d_f), tok(d_f), tok(d_l), tok(d_l)],
        out_shape=[jax.ShapeDtypeStruct((t, d_f), bf16), jax.ShapeDtypeStruct((t, d_f), bf16),
                   jax.ShapeDtypeStruct((t, d_l), f32), jax.ShapeDtypeStruct((t, d_l), f32)],
        compiler_params=_cparams("arbitrary"),
        name="in_proj",
    )(*x_args, mods_l, mods_l, g, w_in, ab)


def _fourier_ctx_kernel(pa_ref, pb_ref, c_ref, ms_ref, o_ref):
    for b in range(pa_ref.shape[0]):
        y = jnp.dot(c_ref[...], pa_ref[b], preferred_element_type=f32)
        y = y + jnp.dot(ms_ref[...], pb_ref[b], preferred_element_type=f32)
        o_ref[b] = y.astype(bf16)


def _fourier_ctx(pa, pb, cpos, mspos, n_batch, seq):
    t, d_f = pa.shape
    nb = 4
    view = (t // seq, seq, d_f)
    blk = pl.BlockSpec((nb, seq, d_f), lambda g: (g, 0, 0))
    return pl.pallas_call(
        _fourier_ctx_kernel,
        grid=(n_batch // nb,),
        in_specs=[blk, blk, _const_spec((seq, seq)), _const_spec((seq, seq))],
        out_specs=blk,
        out_shape=jax.ShapeDtypeStruct((n_batch, seq, d_f), bf16),
        compiler_params=_cparams("arbitrary"),
        name="fourier_pos_ctx",
    )(pa.reshape(view), pb.reshape(view), cpos, mspos)


def _fourier_lat_kernel(pa_ref, pb_ref, c_ref, ms_ref, o_ref):
    y = jnp.dot(c_ref[...], pa_ref[0], preferred_element_type=f32)
    y = y + jnp.dot(ms_ref[...], pb_ref[0], preferred_element_type=f32)
    o_ref[0] = y.astype(bf16)


def _fourier_lat(pa, pb, c2, ms2, n_batch, seq, batch0):
    t, d_f = pa.shape
    view = (t // seq, seq, d_f)
    tq = min(FOURIER_TQ, seq)
    pblk = pl.BlockSpec((1, seq, d_f), lambda b, q: (batch0 + b, 0, 0))
    cblk = pl.BlockSpec((tq, seq), lambda b, q: (q, 0))
    return pl.pallas_call(
        _fourier_lat_kernel,
        grid=(n_batch, seq // tq),
        in_specs=[pblk, pblk, cblk, cblk],
        out_specs=pl.BlockSpec((1, tq, d_f), lambda b, q: (b, q, 0)),
        out_shape=jax.ShapeDtypeStruct((n_batch, seq, d_f), bf16),
        compiler_params=_cparams("arbitrary", "arbitrary"),
        name="fourier_pos_lat",
    )(pa.reshape(view), pb.reshape(view), c2, ms2)


def _gelu_tanh(x):
    return x * (0.5 * (1.0 + jnp.tanh(math.sqrt(2.0 / math.pi) * (x + 0.044715 * (x * x * x)))))


def _lru_kernel(*refs, direction, combine, nb, lc, ch, pitch, n_chunks):
    n_slab = ch // LANES
    (ux, hp, hn, cw, cb, wg, bg, lam, h0) = refs[:9]
    if combine:
        (hf, ug, y_out, st_out) = refs[9:13]
    else:
        (hf_out, st_out) = refs[9:11]
    a_scr = refs[-2 * n_slab - 1:-n_slab - 1]
    b_scr = refs[-n_slab - 1:-1]
    car = refs[-1]
    c = pl.program_id(2)
    cc = c if direction == 0 else n_chunks - 1 - c

    @pl.when(c == 0)
    def _():
        car[...] = h0[...]

    row = lax.broadcasted_iota(i32, (lc, ch), 0)
    z = -lam[...]
    softplus = jnp.maximum(z, 0.0) + jnp.log1p(jnp.exp(-jnp.abs(z)))
    for b in range(nb):
        x = ux[b]
        zero = jnp.zeros((1, ch), f32)
        p6 = jnp.where(cc > 0, hp[b, 6:7, :], zero)
        p7 = jnp.where(cc > 0, hp[b, 7:8, :], zero)
        n0 = jnp.where(cc < n_chunks - 1, hn[b, 0:1, :], zero)
        xm1 = jnp.where(row == 0, p7, pltpu.roll(x, 1, 0))
        xm2 = jnp.where(row == 0, p6, jnp.where(row == 1, p7, pltpu.roll(x, 2, 0)))
        xp1 = jnp.where(row == lc - 1, n0, pltpu.roll(x, lc - 1, 0))
        xc = cb[...] + xm2 * cw[0:1, :] + xm1 * cw[1:2, :] + x * cw[2:3, :] + xp1 * cw[3:4, :]
        gates = jnp.dot(xc.astype(bf16), wg[0], preferred_element_type=f32) + bg[0]
        r = _sigmoid(gates[:, :ch])
        i = _sigmoid(gates[:, ch:])
        log_a = (-RG_LRU_C * r) * softplus
        a = jnp.exp(log_a)
        th = jnp.tanh(log_a)
        mult = jnp.sqrt((-2.0 * th) / (1.0 - th))
        bx = mult * (i * xc)
        for s in range(n_slab):
            a_scr[s][b * pitch:b * pitch + lc, :] = a[:, s * LANES:(s + 1) * LANES]
            b_scr[s][b * pitch:b * pitch + lc, :] = bx[:, s * LANES:(s + 1) * LANES]

    def steps(k, hs):
        base = k * SCAN_UNROLL if direction == 0 else lc - (k + 1) * SCAN_UNROLL
        base = pl.multiple_of(base, SCAN_UNROLL)
        hs = list(hs)
        for j in range(SCAN_UNROLL):
            t = base + (j if direction == 0 else SCAN_UNROLL - 1 - j)
            for s in range(n_slab):
                idx = pl.ds(t, nb, stride=pitch)
                hs[s] = a_scr[s][idx, :] * hs[s] + b_scr[s][idx, :]
                b_scr[s][idx, :] = hs[s]
        return tuple(hs)

    hs = lax.fori_loop(0, lc // SCAN_UNROLL, steps,
                       tuple(car[:, s * LANES:(s + 1) * LANES] for s in range(n_slab)))
    for s in range(n_slab):
        car[:, s * LANES:(s + 1) * LANES] = hs[s]
    st_out[...] = car[...]

    for b in range(nb):
        for s in range(n_slab):
            sl = slice(s * LANES, (s + 1) * LANES)
            h = b_scr[s][b * pitch:b * pitch + lc, :]
            if combine:
                y_out[b, :, sl] = ((hf[b, :, sl] + h) * _gelu_tanh(ug[b, :, sl])).astype(bf16)
            else:
                hf_out[b, :, sl] = h


def _lru_pass(direction, ux, ug, hf, conv_w, conv_b, wg, bg, lam, h0, n_batch, seq, batch0, nb):
    t, d_l = ux.shape
    ch = d_l // N_HEADS
    lc = min(LRU_CHUNK, seq)
    n_chunks = seq // lc
    pitch = lc + SUBLANES
    view = (t // seq, seq, d_l)
    assert batch0 % nb == 0 and n_batch % nb == 0
    g0 = batch0 // nb
    combine = direction == 1
    pos = (lambda c: c) if direction == 0 else (lambda c: n_chunks - 1 - c)
    l8 = lc // SUBLANES
    n8 = seq // SUBLANES
    main = pl.BlockSpec((nb, lc, ch), lambda g, h, c: (g0 + g, pos(c), h))
    prev = pl.BlockSpec((nb, SUBLANES, ch), lambda g, h, c: (g0 + g, jnp.maximum(pos(c) * l8 - 1, 0), h))
    nxt = pl.BlockSpec((nb, SUBLANES, ch), lambda g, h, c: (g0 + g, jnp.minimum((pos(c) + 1) * l8, n8 - 1), h))
    local = pl.BlockSpec((nb, lc, ch), lambda g, h, c: (g, pos(c), h))
    head_row = lambda rows: pl.BlockSpec((rows, ch), lambda g, h, c: (0, h))
    state = pl.BlockSpec((nb, ch), lambda g, h, c: (g, h))
    in_specs = [main, prev, nxt, head_row(CONV_W), head_row(1),
                pl.BlockSpec((1, ch, 2 * ch), lambda g, h, c: (h, 0, 0)),
                pl.BlockSpec((1, 1, 2 * ch), lambda g, h, c: (h, 0, 0)),
                head_row(1), state]
    ux3 = ux.reshape(view)
    args = [ux3, ux3, ux3, conv_w, conv_b, wg, bg, lam, h0]
    scratch = [pltpu.VMEM((nb * pitch, LANES), f32) for _ in range(2 * (ch // LANES))]
    scratch.append(pltpu.VMEM((nb, ch), f32))
    st_shape = jax.ShapeDtypeStruct((n_batch, d_l), f32)
    if combine:
        in_specs += [local, main]
        args += [hf, ug.reshape(view)]
    out_specs = [local, state]
    out_shape = [jax.ShapeDtypeStruct((n_batch, seq, d_l), bf16 if combine else f32), st_shape]
    kern = functools.partial(_lru_kernel, direction=direction, combine=combine,
                             nb=nb, lc=lc, ch=ch, pitch=pitch, n_chunks=n_chunks)
    return pl.pallas_call(
        kern,
        grid=(n_batch // nb, N_HEADS, n_chunks),
        in_specs=in_specs, out_specs=out_specs, out_shape=out_shape,
        scratch_shapes=scratch,
        compiler_params=_cparams("arbitrary", "arbitrary", "arbitrary"),
        name="rglru_dir%d" % direction,
    )(*args)


def _outproj_kernel(*refs, d_f, n_ctx_tiles, n_x, routed):
    (g1, sh2, sc2, gnf, gnl, n2g, wo) = refs[4 + n_x:11 + n_x]
    if routed:
        (rw, rb, x1_out, hm_out, lg_out) = refs[11 + n_x:]
    else:
        (x1_out, hm_out) = refs[11 + n_x:]
    yf = _stream_value(refs[0:2], n_ctx_tiles)
    yl = _stream_value(refs[2:4], n_ctx_tiles)
    x = _stream_value(refs[4:4 + n_x], n_ctx_tiles)
    yfn = _rms(yf.astype(f32), gnf[...]).astype(bf16)
    yln = _rms(yl.astype(f32), gnl[...]).astype(bf16)
    out = jnp.dot(yfn, wo[:d_f, :], preferred_element_type=f32)
    out = out + jnp.dot(yln, wo[d_f:, :], preferred_element_type=f32)
    x1 = x + g1[0] * out
    x1_out[...] = x1
    hm = _rms(x1, n2g[...]) * (1.0 + sc2[0]) + sh2[0]
    hm_out[...] = hm.astype(hm_out.dtype)
    if routed:
        ne = rw.shape[1] // 2
        hi = hm.astype(bf16)
        lo = (hm - hi.astype(f32)).astype(bf16)
        big = jnp.dot(hi, rw[...], preferred_element_type=f32)
        small = jnp.dot(lo, rw[:, :ne], preferred_element_type=f32)
        lg_out[...] = big[:, :ne] + (big[:, ne:] + small) + rb[...]


def _out_proj(yf, yl, x, t, mods_l, gnf, gnl, n2g, w_out, cond_of_tile, router):
    d = w_out.shape[1]
    d_f = yf[0].shape[1]
    d_l = yl[0].shape[1]
    nct = yf[0].shape[0] // TM
    routed = router is not None
    mod = lambda j: pl.BlockSpec((1, 1, d), lambda i: (cond_of_tile(i), 0, j))
    tok = lambda n: pl.BlockSpec((TM, n), lambda i: (i, 0))
    yf_specs, yf_args = _stream_specs(yf, d_f, nct)
    yl_specs, yl_args = _stream_specs(yl, d_l, nct)
    x_specs, x_args = _stream_specs(x, d, nct)
    in_specs = yf_specs + yl_specs + x_specs + [
        mod(2), mod(3), mod(4), _const_spec((1, d_f)), _const_spec((1, d_l)), _const_spec((1, d)),
        _const_spec((d_f + d_l, d))]
    args = yf_args + yl_args + x_args + [mods_l, mods_l, mods_l, gnf, gnl, n2g, w_out]
    out_specs = [tok(d), tok(d)]
    out_shape = [jax.ShapeDtypeStruct((t, d), f32), jax.ShapeDtypeStruct((t, d), f32 if routed else bf16)]
    if routed:
        rw, rb = router
        in_specs += [_const_spec(rw.shape), _const_spec(rb.shape)]
        args += [rw, rb]
        out_specs.append(tok(LANES))
        out_shape.append(jax.ShapeDtypeStruct((t, LANES), f32))
    return pl.pallas_call(
        functools.partial(_outproj_kernel, d_f=d_f, n_ctx_tiles=nct, n_x=len(x_args), routed=routed),
        grid=(t // TM,),
        in_specs=in_specs, out_specs=out_specs, out_shape=out_shape,
        compiler_params=_cparams("arbitrary"),
        name="out_proj",
    )(*args)


def _ffn_kernel(*refs, final, n_ctx_tiles):
    if final:
        hm, x1, g2, wg, wu, wd, fg = refs[:7]
        o_refs = refs[7:9]
    else:
        hm, x1, g2, wg, wu, wd = refs[:6]
        o_refs = refs[6:7]
    acc = refs[-1] if final else o_refs[0]
    f = pl.program_id(1)

    @pl.when(f == 0)
    def _():
        acc[...] = jnp.zeros(acc.shape, f32)

    xb = hm[...]
    g = jnp.dot(xb, wg[...], preferred_element_type=f32)
    u = jnp.dot(xb, wu[...], preferred_element_type=f32)
    h = ((g * _sigmoid(g)) * u).astype(bf16)
    acc[...] += jnp.dot(h, wd[...], preferred_element_type=f32)

    @pl.when(f == pl.num_programs(1) - 1)
    def _():
        y = x1[...] + g2[0] * acc[...]
        _stream_store(o_refs, _rms(y, fg[...]) if final else y, n_ctx_tiles)


def _final_out(t, t_ctx, d, rows, final):
    shapes = (jax.ShapeDtypeStruct((t_ctx, d), f32), jax.ShapeDtypeStruct((t - t_ctx, d), f32))
    specs, _ = _stream_specs(shapes if final else None, d, t_ctx // rows, rows)
    return specs, (list(shapes) if final else [jax.ShapeDtypeStruct((t, d), f32)])


def _ffn_dense(hm, x1, mods_l, wg, wu, wd, m, cond_of_tile, final_g, t_ctx):
    t, d = x1.shape
    dff = wg.shape[2]
    final = final_g is not None
    tmd = TMD_FINAL if final else TMD
    tok = pl.BlockSpec((tmd, d), lambda i, f: (i, 0))
    in_specs = [tok, tok, pl.BlockSpec((1, 1, d), lambda i, f: (cond_of_tile(tmd)(i), 0, 5)),
                pl.BlockSpec((None, d, TFD), lambda i, f: (m, 0, f)),
                pl.BlockSpec((None, d, TFD), lambda i, f: (m, 0, f)),
                pl.BlockSpec((None, TFD, d), lambda i, f: (m, f, 0))]
    args = [hm, x1, mods_l, wg, wu, wd]
    if final:
        in_specs.append(pl.BlockSpec((1, d), lambda i, f: (0, 0)))
        args.append(final_g)
    out_specs, out_shape = _final_out(t, t_ctx, d, tmd, final)
    out = pl.pallas_call(
        functools.partial(_ffn_kernel, final=final, n_ctx_tiles=t_ctx // tmd),
        grid=(t // tmd, dff // TFD),
        in_specs=in_specs, out_specs=out_specs, out_shape=out_shape,
        scratch_shapes=[pltpu.VMEM((tmd, d), f32)] if final else [],
        compiler_params=pltpu.CompilerParams(dimension_semantics=("arbitrary", "arbitrary"),
                                             vmem_limit_bytes=MOE_VMEM_LIMIT_BYTES),
        name="ffn_dense",
    )(*args)
    return tuple(out) if final else out[0]


def _route_kernel(lg_ref, meta_ref, cnt_ref, run, *, n_exp):
    i = pl.program_id(0)
    tr = lg_ref.shape[0]

    @pl.when(i == 0)
    def _():
        run[...] = jnp.zeros(run.shape, f32)

    lane = lax.broadcasted_iota(i32, (tr, LANES), 1)
    lane_f = lane.astype(f32)
    neg = jnp.float32(-jnp.inf)
    lg = jnp.where(lane < n_exp, lg_ref[...], neg)
    m1 = jnp.max(lg, axis=1, keepdims=True)
    i1 = jnp.min(jnp.where(lg == m1, lane_f, float(LANES)), axis=1, keepdims=True)
    oh1 = lane_f == i1
    lg2 = jnp.where(oh1, neg, lg)
    m2 = jnp.max(lg2, axis=1, keepdims=True)
    i2 = jnp.min(jnp.where(lg2 == m2, lane_f, float(LANES)), axis=1, keepdims=True)
    oh2 = lane_f == i2
    e = jnp.exp(m2 - m1)
    g1 = 1.0 / (1.0 + e)
    g2 = e / (1.0 + e)
    sel = jnp.where(oh1 | oh2, 1.0, 0.0)
    rr = lax.broadcasted_iota(i32, (tr, tr), 0)
    rc = lax.broadcasted_iota(i32, (tr, tr), 1)
    tri = jnp.where(rc < rr, 1.0, 0.0).astype(bf16)
    pos = jnp.dot(tri, sel.astype(bf16), preferred_element_type=f32) + run[...]
    p1 = jnp.sum(jnp.where(oh1, pos, 0.0), axis=1, keepdims=True)
    p2 = jnp.sum(jnp.where(oh2, pos, 0.0), axis=1, keepdims=True)
    run[...] = run[...] + jnp.sum(sel, axis=0, keepdims=True)
    cnt_ref[...] = run[...]
    cols = (i1, i2, p1, p2, g1, g2)
    meta = jnp.zeros((tr, LANES), f32)
    for k, v in enumerate(cols):
        meta = jnp.where(lane == k, v, meta)
    meta_ref[...] = meta


def _route(logits, n_exp):
    t = logits.shape[0]
    return pl.pallas_call(
        functools.partial(_route_kernel, n_exp=n_exp),
        grid=(t // TM,),
        in_specs=[pl.BlockSpec((TM, LANES), lambda i: (i, 0))],
        out_specs=[pl.BlockSpec((TM, LANES), lambda i: (i, 0)), pl.BlockSpec((1, LANES), lambda i: (0, 0))],
        out_shape=[jax.ShapeDtypeStruct((t, LANES), f32), jax.ShapeDtypeStruct((1, LANES), f32)],
        scratch_shapes=[pltpu.VMEM((1, LANES), f32)],
        compiler_params=_cparams("arbitrary"),
        name="route_top2",
    )(logits)


def _row_copy(src, src_row, dst, dst_row, sem):
    return pltpu.make_async_copy(src.at[pl.ds(src_row, 1), :], dst.at[pl.ds(dst_row, 1), :], sem)


def _scatter_kernel(d1, d2, zt, hm_ref, xs_out, zbuf, sem, zsem):
    i = pl.program_id(0)

    @pl.when(i == 0)
    def _():
        zbuf[...] = jnp.zeros(zbuf.shape, f32)
        n_fill = TMM // zbuf.shape[0]

        def fills(k, start):
            fresh = (k == 0) | (zt[k] != zt[jnp.maximum(k - 1, 0)])

            @pl.when(fresh)
            def _():
                for q in range(n_fill):
                    dst = xs_out.at[pl.ds(zt[k] * TMM + q * zbuf.shape[0], zbuf.shape[0]), :]
                    cp = pltpu.make_async_copy(zbuf, dst, zsem.at[0])
                    cp.start() if start else cp.wait()

        lax.fori_loop(0, zt.shape[0], lambda k, c: (fills(k, True), c)[1], 0)
        lax.fori_loop(0, zt.shape[0], lambda k, c: (fills(k, False), c)[1], 0)

    def issue(r, c):
        t = i * TROW + r
        _row_copy(hm_ref, r, xs_out, d1[t], sem.at[0]).start(priority=0)
        _row_copy(hm_ref, r, xs_out, d2[t], sem.at[1]).start(priority=1)
        return c

    def drain(r, c):
        _row_copy(hm_ref, 0, xs_out, 0, sem.at[0]).wait()
        _row_copy(hm_ref, 0, xs_out, 0, sem.at[1]).wait()
        return c

    lax.fori_loop(0, TROW, issue, 0, unroll=ROW_UNROLL)
    lax.fori_loop(0, TROW, drain, 0, unroll=ROW_UNROLL)


def _scatter_rows(d1, d2, zero_tiles, hm, n_rows):
    t, d = hm.shape
    return pl.pallas_call(
        _scatter_kernel,
        grid_spec=pltpu.PrefetchScalarGridSpec(
            num_scalar_prefetch=3, grid=(t // TROW,),
            in_specs=[pl.BlockSpec((TROW, d), lambda i, a, b, z: (i, 0))],
            out_specs=pl.BlockSpec(memory_space=pl.ANY),
            scratch_shapes=[pltpu.VMEM((TROW, d), f32), pltpu.SemaphoreType.DMA((2,)),
                            pltpu.SemaphoreType.DMA((1,))]),
        out_shape=jax.ShapeDtypeStruct((n_rows, d), f32),
        compiler_params=_cparams("arbitrary"),
        name="moe_scatter_rows",
    )(d1, d2, zero_tiles, hm)


def _moe_kernel(te, nu, nv, xs, wg, wu, wd, o_ref, xb):
    del te, nu
    j = pl.program_id(0)
    f = pl.program_id(1)
    rows = nv[j]

    @pl.when(f == 0)
    def _():
        o_ref[...] = jnp.zeros(o_ref.shape, f32)

    @pl.when((f == 0) & (rows > 0))
    def _():
        xb[...] = xs[...].astype(bf16)

    def swiglu_rows(n_rows):
        x = xb[:n_rows, :]
        g = jnp.dot(x, wg[0].astype(bf16), preferred_element_type=f32)
        u = jnp.dot(x, wu[0].astype(bf16), preferred_element_type=f32)
        h = ((g * _sigmoid(g)) * u).astype(bf16)
        o_ref[:n_rows, :] += jnp.dot(h, wd[0].astype(bf16), preferred_element_type=f32)

    n_piece = xb.shape[0] // MOE_SUB
    for p in range(1, n_piece + 1):
        @pl.when((rows > (p - 1) * MOE_SUB) & (rows <= p * MOE_SUB))
        def _(p=p):
            swiglu_rows(p * MOE_SUB)


def _moe_grouped(tile_expert, n_used, tile_rows, xs, wg, wu, wd, m):
    r, d = xs.shape
    dff = wg.shape[3]
    nf = dff // TFM
    row = lambda j, f, te, nu, nv: (jnp.minimum(j, nu[0] - 1), 0)
    fidx = lambda j, f, nu: jnp.where(j < nu[0], f, nf - 1)
    return pl.pallas_call(
        _moe_kernel,
        grid_spec=pltpu.PrefetchScalarGridSpec(
            num_scalar_prefetch=3, grid=(r // TMM, nf),
            in_specs=[pl.BlockSpec((TMM, d), row, pipeline_mode=pl.Buffered(MOE_XS_BUFFERS)),
                      pl.BlockSpec((None, 1, d, TFM), lambda j, f, te, nu, nv: (m, te[j], 0, fidx(j, f, nu))),
                      pl.BlockSpec((None, 1, d, TFM), lambda j, f, te, nu, nv: (m, te[j], 0, fidx(j, f, nu))),
                      pl.BlockSpec((None, 1, TFM, d), lambda j, f, te, nu, nv: (m, te[j], fidx(j, f, nu), 0))],
            out_specs=pl.BlockSpec((TMM, d), lambda j, f, te, nu, nv: (j, 0)),
            scratch_shapes=[pltpu.VMEM((TMM, d), bf16)]),
        out_shape=jax.ShapeDtypeStruct((r, d), f32),
        compiler_params=pltpu.CompilerParams(dimension_semantics=("arbitrary", "arbitrary"),
                                             vmem_limit_bytes=MOE_VMEM_LIMIT_BYTES),
        name="moe_grouped_swiglu",
    )(tile_expert, n_used, tile_rows, xs, wg, wu, wd)


def _combine_kernel(*refs, final, n_ctx_tiles):
    if final:
        d1, d2, x1, g2, meta, fg, ys = refs[:7]
        o_refs = refs[7:9]
    else:
        d1, d2, x1, g2, meta, ys = refs[:6]
        o_refs = refs[6:7]
    buf, sem = refs[-2:]
    i = pl.program_id(0)
    slot = i % 2

    def issue(step, s):
        def body(r, c):
            t = step * TROW + r
            _row_copy(ys, d1[t], buf.at[s, 0], r, sem.at[s, 0]).start(priority=0)
            _row_copy(ys, d2[t], buf.at[s, 1], r, sem.at[s, 1]).start(priority=1)
            return c
        lax.fori_loop(0, TROW, body, 0, unroll=ROW_UNROLL)

    def drain(r, c):
        _row_copy(ys, 0, buf.at[slot, 0], 0, sem.at[slot, 0]).wait()
        _row_copy(ys, 0, buf.at[slot, 1], 0, sem.at[slot, 1]).wait()
        return c

    @pl.when(i == 0)
    def _():
        issue(0, 0)

    @pl.when(i + 1 < pl.num_programs(0))
    def _():
        issue(i + 1, 1 - slot)

    lax.fori_loop(0, TROW, drain, 0, unroll=ROW_UNROLL)
    m = meta[...]
    ff = m[:, 4:5] * buf[slot, 0] + m[:, 5:6] * buf[slot, 1]
    y = x1[...] + g2[0] * ff
    _stream_store(o_refs, _rms(y, fg[...]) if final else y, n_ctx_tiles)


def _combine_rows(d1, d2, x1, mods_l, meta, ys, cond_of_tile_row, final_g, t_ctx):
    t, d = x1.shape
    final = final_g is not None
    tok = lambda n: pl.BlockSpec((TROW, n), lambda i, a, b: (i, 0))
    in_specs = [tok(d), pl.BlockSpec((1, 1, d), lambda i, a, b: (cond_of_tile_row(i), 0, 5)), tok(LANES)]
    args = [x1, mods_l, meta]
    if final:
        in_specs.append(pl.BlockSpec((1, d), lambda i, a, b: (0, 0)))
        args.append(final_g)
    in_specs.append(pl.BlockSpec(memory_space=pl.ANY))
    args.append(ys)
    out_specs, out_shape = _final_out(t, t_ctx, d, TROW, final)
    out = pl.pallas_call(
        functools.partial(_combine_kernel, final=final, n_ctx_tiles=t_ctx // TROW),
        grid_spec=pltpu.PrefetchScalarGridSpec(
            num_scalar_prefetch=2, grid=(t // TROW,),
            in_specs=in_specs, out_specs=out_specs,
            scratch_shapes=[pltpu.VMEM((2, TOP_K, TROW, d), f32), pltpu.SemaphoreType.DMA((2, TOP_K))]),
        out_shape=out_shape,
        compiler_params=_cparams("arbitrary"),
        name="moe_combine_rows",
    )(d1, d2, *args)
    return tuple(out) if final else out[0]


def _moe_ffn(hm, logits, x1, mods_l, wg, wu, wd, m, cond_of_tile_row, final_g, t_ctx):
    t, d = hm.shape
    n_exp = wg.shape[1]
    assert (TOP_K * t) % TMM == 0 and TMM % MOE_SUB == 0
    meta, counts = _route(logits, n_exp)
    cnt = counts[0, :n_exp].astype(i32)
    padded = ((cnt + TMM - 1) // TMM) * TMM
    ends = jnp.cumsum(padded)
    off = ends - padded
    e1 = meta[:, 0].astype(i32)
    e2 = meta[:, 1].astype(i32)
    d1 = off[e1] + meta[:, 2].astype(i32)
    d2 = off[e2] + meta[:, 3].astype(i32)
    n_tiles = (TOP_K * t) // TMM + n_exp
    tile_start = jnp.arange(n_tiles, dtype=i32) * TMM
    n_used = (ends[-1] // TMM).reshape(1).astype(i32)
    tile_expert = jnp.minimum(jnp.sum((tile_start[:, None] >= ends[None, :]).astype(i32), axis=1), n_exp - 1)
    tile_rows = jnp.clip((off + cnt)[tile_expert] - tile_start, 0, TMM)
    tile_rows = jnp.where(tile_start < ends[-1], tile_rows, 0)
    tile_expert = jnp.where(tile_start < ends[-1], tile_expert, tile_expert[n_used[0] - 1])
    last_tile = jnp.maximum(ends - 1, 0) // TMM
    spare_tile = jnp.minimum(n_used[0] + jnp.arange(n_exp, dtype=i32), n_tiles - 1)
    zero_tiles = jnp.concatenate([last_tile, jnp.maximum(spare_tile, last_tile[-1])]).astype(i32)
    xs = _scatter_rows(d1, d2, zero_tiles, hm, n_tiles * TMM)
    ys = _moe_grouped(tile_expert, n_used, tile_rows, xs, wg, wu, wd, m)
    return _combine_rows(d1, d2, x1, mods_l, meta, ys, cond_of_tile_row, final_g, t_ctx)


@functools.lru_cache(maxsize=None)
def _dft_tables(n_ctx, rows, grid_w, dh):
    def cos_sin(k, period):
        ang = 2.0 * np.pi * (k % period).astype(np.float64) / period
        return np.cos(ang), np.sin(ang)

    k = np.arange(dh)
    cc, sc = cos_sin(k[:, None] * k[None, :], dh)
    chan = np.stack([cc, sc]).astype(np.float32)
    t = np.arange(n_ctx)
    cp, sp = cos_sin(t[:, None] * t[None, :], n_ctx)
    s_ctx = 1.0 / math.sqrt(n_ctx * dh)
    n = np.arange(rows * grid_w)
    r, c = n // grid_w, n % grid_w
    period = rows * grid_w // math.gcd(rows, grid_w)
    phase = (r[:, None] * r[None, :]) * (period // rows) + (c[:, None] * c[None, :]) * (period // grid_w)
    c2, s2 = cos_sin(phase, period)
    s_lat = 1.0 / math.sqrt(rows * grid_w * dh)
    as_f32 = lambda a: np.asarray(a, np.float32)
    return chan, as_f32(cp * s_ctx), as_f32(-sp * s_ctx), as_f32(c2 * s_lat), as_f32(-s2 * s_lat)


def kernel(x_prompt, x_sample, state_lru, c, c_ctx, norm1_g, norm2_g, w_mod, b_mod, w_in, w_fourier,
           gn_fourier_g, conv_w, conv_b, w_r, b_r, w_i, b_i, lam, gn_lru_g, w_out, ffn_w_gate, ffn_w_up,
           ffn_w_down, router_w, router_b, moe_w_gate, moe_w_up, moe_w_down, final_g):
    nb_ctx, seq_ctx, d = x_prompt.shape
    nb_lat, seq_lat, _ = x_sample.shape
    depth = w_mod.shape[0]
    d_f = w_fourier.shape[1] * w_fourier.shape[2]
    d_l = lam.shape[2]
    dh = d_f // N_HEADS
    t_ctx = nb_ctx * seq_ctx
    t_lat = nb_lat * seq_lat
    assert t_ctx % TMD == 0 and seq_lat % TMD == 0 and TMD % TM == 0
    assert t_ctx % seq_lat == 0 and seq_lat % GRID_W == 0
    assert d_l // N_HEADS == dh and dh % LANES == 0 and TM % TROW == 0

    def cond_of(tile_rows):
        return lambda i: jnp.where(i * tile_rows < t_ctx, 0,
                                   1 + jnp.maximum(i * tile_rows - t_ctx, 0) // seq_lat)

    cond_tm, cond_row = cond_of(TM), cond_of(TROW)

    n_cond = 1 + nb_lat
    ncp = -(-n_cond // SUBLANES) * SUBLANES
    cond = jnp.concatenate([c_ctx[None, :], c, jnp.zeros((ncp - n_cond, d), f32)], axis=0)
    mods = _mods(cond.T, w_mod, b_mod, n_cond).reshape(depth, ncp, 1, N_MOD * d)

    chan, cp, msp, c2, ms2 = _dft_tables(seq_ctx, seq_lat // GRID_W, GRID_W, dh)
    chan = jnp.asarray(chan)
    cp, msp, c2, ms2 = (jnp.asarray(a).astype(bf16) for a in (cp, msp, c2, ms2))

    t = t_ctx + t_lat
    x = (x_prompt.reshape(t_ctx, d), x_sample.reshape(t_lat, d))
    zeros_h0 = jnp.zeros((nb_ctx, d_l), f32)
    ctx_states = []
    for l in range(depth):
        mods_l = mods[l]
        ab = _fold_channel_dft(chan, w_fourier[l])
        pa, pb, ux, ug = _in_proj(x, t, t_ctx // TM, mods_l, norm1_g[l][None, :], w_in[l].astype(bf16), ab,
                                  cond_tm, d_f, d_l)

        yf_c = _fourier_ctx(pa, pb, cp, msp, nb_ctx, seq_ctx).reshape(t_ctx, d_f)
        yf_l = _fourier_lat(pa, pb, c2, ms2, nb_lat, seq_lat, t_ctx // seq_lat).reshape(t_lat, d_f)

        def gate_w(dr):
            wgt = jnp.concatenate([w_r[l, dr], w_i[l, dr]], axis=-1).astype(bf16)
            bias = jnp.concatenate([b_r[l, dr].reshape(N_HEADS, 1, dh), b_i[l, dr].reshape(N_HEADS, 1, dh)], axis=-1)
            return wgt, bias

        yls = []
        states = []
        for (n_batch, seq, batch0, nb, h0) in (
                (nb_ctx, seq_ctx, 0, SUBLANES, (zeros_h0, zeros_h0)),
                (nb_lat, seq_lat, t_ctx // seq_lat, nb_lat,
                 (state_lru[:, l, 0], state_lru[:, l, 1]))):
            wg0, bg0 = gate_w(0)
            hf, st_f = _lru_pass(0, ux, None, None, conv_w[l], conv_b[l][None, :], wg0, bg0,
                                 lam[l, 0][None, :], h0[0], n_batch, seq, batch0, nb)
            wg1, bg1 = gate_w(1)
            yl, st_b = _lru_pass(1, ux, ug, hf, conv_w[l], conv_b[l][None, :], wg1, bg1,
                                 lam[l, 1][None, :], h0[1], n_batch, seq, batch0, nb)
            yls.append(yl.reshape(n_batch * seq, d_l))
            states.append((st_f, st_b))
        ctx_states.append(jnp.stack(states[0], axis=1))
        mixed = ((yf_c, yf_l), (yls[0], yls[1]))

        m = l // 2
        last = l == depth - 1
        fg = final_g[None, :] if last else None
        if l % 2 == 0:
            x1, hm = _out_proj(*mixed, x, t, mods_l, gn_fourier_g[l][None, :], gn_lru_g[l][None, :],
                               norm2_g[l][None, :], w_out[l].astype(bf16), cond_tm, None)
            x = _ffn_dense(hm, x1, mods_l, ffn_w_gate.astype(bf16), ffn_w_up.astype(bf16),
                           ffn_w_down.astype(bf16), m, cond_of, fg, t_ctx)
        else:
            n_exp = router_w.shape[2]
            rw = jnp.zeros((d, LANES), f32).at[:, :n_exp].set(router_w[m])
            rw_hi = rw.astype(bf16)
            rw_lo = (rw - rw_hi.astype(f32)).astype(bf16)
            rb = jnp.zeros((1, LANES), f32).at[0, :n_exp].set(router_b[m])
            x1, hm, logits = _out_proj(*mixed, x, t, mods_l, gn_fourier_g[l][None, :], gn_lru_g[l][None, :],
                                       norm2_g[l][None, :], w_out[l].astype(bf16), cond_tm,
                                       (jnp.concatenate([rw_hi, rw_lo], axis=1), rb))
            x = _moe_ffn(hm, logits, x1, mods_l, moe_w_gate, moe_w_up, moe_w_down, m, cond_row, fg, t_ctx)
    y_prompt = x[0].reshape(nb_ctx, seq_ctx, d)
    y_sample = x[1].reshape(nb_lat, seq_lat, d)
    new_state = jnp.stack(ctx_states, axis=1).astype(state_lru.dtype)
    return (y_prompt, y_sample, new_state)
```

```python
import functools
import math

import jax
import jax.numpy as jnp
import numpy as np
from jax import lax
from jax.experimental import pallas as pl
from jax.experimental.pallas import tpu as pltpu

f32 = jnp.float32
bf16 = jnp.bfloat16
i32 = jnp.int32

GRID_W = 64
N_HEADS = 4
CONV_W = 4
CONV_PAD_LEFT = 2
RG_LRU_C = 8.0
N_MOD = 6
TOP_K = 2
EPS = 1e-6

LANES = 128
SUBLANES = 8
VMEM_LIMIT_BYTES = 56 * 1024 * 1024

MOE_VMEM_LIMIT_BYTES = 60 * 1024 * 1024

TM = 512
TMD = 1024
TMD_FINAL = 512
TFD = 512
TMM = 1024
MOE_SUB = 512
TFM = 256
MOE_XS_BUFFERS = 2
LRU_CHUNK = 256
SCAN_UNROLL = 8
TROW = 256
ROW_UNROLL = 8
FOURIER_TQ = 512


def _cparams(*sem):
    return pltpu.CompilerParams(dimension_semantics=sem, vmem_limit_bytes=VMEM_LIMIT_BYTES)


def _rms(x, g):
    return x * lax.rsqrt(jnp.mean(x * x, axis=-1, keepdims=True) + EPS) * g


def _sigmoid(x):
    return 1.0 / (1.0 + jnp.exp(-x))


def _const_spec(shape):
    nd = len(shape)
    return pl.BlockSpec(shape, lambda *_: (0,) * nd, pipeline_mode=pl.Buffered(1))


def _mod_kernel(ct_ref, w_ref, b_ref, o_ref, sb, *, n_cond, kc):
    d = ct_ref.shape[0]
    n_lt = w_ref.shape[2] // LANES

    @pl.when((pl.program_id(0) == 0) & (pl.program_id(1) == 0))
    def _():
        c = ct_ref[...]
        s = c * _sigmoid(c)
        for r in range(n_cond):
            sb[r] = jnp.broadcast_to(s[:, r:r + 1], (d, LANES))

    def body(k, accs):
        k0 = pl.multiple_of(k * kc, kc)
        ws = [w_ref[0, pl.ds(k0, kc), lt * LANES:(lt + 1) * LANES] for lt in range(n_lt)]
        out = []
        for r in range(n_cond):
            sr = sb[r, pl.ds(k0, kc), :]
            for lt in range(n_lt):
                part = (ws[lt] * sr).reshape(kc // SUBLANES, SUBLANES, LANES).sum(axis=0)
                out.append(accs[r * n_lt + lt] + part)
        return tuple(out)

    zero = jnp.zeros((SUBLANES, LANES), f32)
    accs = lax.fori_loop(0, d // kc, body, tuple(zero for _ in range(n_cond * n_lt)), unroll=2)
    o_ref[...] = jnp.zeros(o_ref.shape, f32)
    for r in range(n_cond):
        for lt in range(n_lt):
            sl = slice(lt * LANES, (lt + 1) * LANES)
            o_ref[0, r:r + 1, sl] = jnp.sum(accs[r * n_lt + lt], axis=0, keepdims=True) + b_ref[0, :, sl]


def _mods(cond_t, w_mod, b_mod, n_cond):
    depth, d, n6 = w_mod.shape
    ncp = cond_t.shape[1]
    tn = 768
    return pl.pallas_call(
        functools.partial(_mod_kernel, n_cond=n_cond, kc=16),
        grid=(depth, n6 // tn),
        in_specs=[pl.BlockSpec((d, ncp), lambda l, j: (0, 0)),
                  pl.BlockSpec((1, d, tn), lambda l, j: (l, 0, j)),
                  pl.BlockSpec((1, 1, tn), lambda l, j: (l, 0, j))],
        out_specs=pl.BlockSpec((1, ncp, tn), lambda l, j: (l, 0, j)),
        out_shape=jax.ShapeDtypeStruct((depth, ncp, n6), f32),
        scratch_shapes=[pltpu.VMEM((n_cond, d, LANES), f32)],
        compiler_params=_cparams("arbitrary", "arbitrary"),
        name="adaln_mods",
    )(cond_t, w_mod, b_mod.reshape(depth, 1, n6))


def _ab_kernel(cs_ref, wf_ref, o_ref):
    dh = wf_ref.shape[1]
    w = wf_ref[0]
    o_ref[0, :, :dh] = jnp.dot(cs_ref[0], w, precision=lax.Precision.HIGHEST,
                               preferred_element_type=f32).astype(bf16)
    o_ref[0, :, dh:] = jnp.dot(cs_ref[1], w, precision=lax.Precision.HIGHEST,
                               preferred_element_type=f32).astype(bf16)


def _fold_channel_dft(cs, w_f):
    nh, dh, _ = w_f.shape
    return pl.pallas_call(
        _ab_kernel,
        grid=(nh,),
        in_specs=[pl.BlockSpec((2, dh, dh), lambda h: (0, 0, 0)),
                  pl.BlockSpec((1, dh, dh), lambda h: (h, 0, 0))],
        out_specs=pl.BlockSpec((1, dh, 2 * dh), lambda h: (h, 0, 0)),
        out_shape=jax.ShapeDtypeStruct((nh, dh, 2 * dh), bf16),
        compiler_params=_cparams("arbitrary"),
        name="fold_channel_dft",
    )(cs, w_f)


def _stream_specs(x, n_cols, n_ctx_tiles, rows=TM):
    if not isinstance(x, tuple):
        return [pl.BlockSpec((rows, n_cols), lambda i, *_: (i, 0))], [x]
    ctx = pl.BlockSpec((rows, n_cols), lambda i, *_: (jnp.minimum(i, n_ctx_tiles - 1), 0))
    lat = pl.BlockSpec((rows, n_cols), lambda i, *_: (jnp.maximum(i - n_ctx_tiles, 0), 0))
    return [ctx, lat], list(x)


def _stream_value(refs, n_ctx_tiles):
    if len(refs) == 1:
        return refs[0][...]
    return jnp.where(pl.program_id(0) < n_ctx_tiles, refs[0][...], refs[1][...])


def _stream_store(refs, y, n_ctx_tiles):
    if len(refs) == 1:
        refs[0][...] = y
        return
    is_ctx = pl.program_id(0) < n_ctx_tiles

    @pl.when(is_ctx)
    def _():
        refs[0][...] = y

    @pl.when(jnp.logical_not(is_ctx))
    def _():
        refs[1][...] = y


def _inproj_kernel(*refs, d_f, d_l, n_x, n_ctx_tiles):
    x = _stream_value(refs[:n_x], n_ctx_tiles)
    (sh_ref, sc_ref, g_ref, w_ref, ab_ref, pa_ref, pb_ref, ux_ref, ug_ref) = refs[n_x:]
    h = _rms(x, g_ref[...]) * (1.0 + sc_ref[0]) + sh_ref[0]
    u = jnp.dot(h.astype(bf16), w_ref[...], preferred_element_type=f32)
    nh = ab_ref.shape[0]
    dh = d_f // nh
    for hh in range(nh):
        sl = slice(hh * dh, (hh + 1) * dh)
        p = jnp.dot(u[:, sl].astype(bf16), ab_ref[hh], preferred_element_type=f32)
        pa_ref[:, sl] = p[:, :dh].astype(bf16)
        pb_ref[:, sl] = p[:, dh:].astype(bf16)
    ux_ref[...] = u[:, d_f:d_f + d_l]
    ug_ref[...] = u[:, d_f + d_l:]


def _in_proj(x, t, n_ctx_tiles, mods_l, g, w_in, ab, cond_of_tile, d_f, d_l):
    d, d_in = w_in.shape
    nh, dh, _ = ab.shape
    mod = lambda j: pl.BlockSpec((1, 1, d), lambda i: (cond_of_tile(i), 0, j))
    tok = lambda n: pl.BlockSpec((TM, n), lambda i: (i, 0))
    x_specs, x_args = _stream_specs(x, d, n_ctx_tiles)
    return pl.pallas_call(
        functools.partial(_inproj_kernel, d_f=d_f, d_l=d_l, n_x=len(x_args), n_ctx_tiles=n_ctx_tiles),
        grid=(t // TM,),
        in_specs=x_specs + [mod(0), mod(1), _const_spec((1, d)), _const_spec((d, d_in)),
                            _const_spec((nh, dh, 2 * dh))],
        out_specs=[tok(d_f), tok(d_f), tok(d_l), tok(d_l)],
        out_shape=[jax.ShapeDtypeStruct((t, d_f), bf16), jax.ShapeDtypeStruct((t, d_f), bf16),
                   jax.ShapeDtypeStruct((t, d_l), f32), jax.ShapeDtypeStruct((t, d_l), f32)],
        compiler_params=_cparams("arbitrary"),
        name="in_proj",
    )(*x_args, mods_l, mods_l, g, w_in, ab)


def _fourier_ctx_kernel(pa_ref, pb_ref, c_ref, ms_ref, o_ref):
    for b in range(pa_ref.shape[0]):
        y = jnp.dot(c_ref[...], pa_ref[b], preferred_element_type=f32)
        y = y + jnp.dot(ms_ref[...], pb_ref[b], preferred_element_type=f32)
        o_ref[b] = y.astype(bf16)


def _fourier_ctx(pa, pb, cpos, mspos, n_batch, seq):
    t, d_f = pa.shape
    nb = 4
    view = (t // seq, seq, d_f)
    blk = pl.BlockSpec((nb, seq, d_f), lambda g: (g, 0, 0))
    return pl.pallas_call(
        _fourier_ctx_kernel,
        grid=(n_batch // nb,),
        in_specs=[blk, blk, _const_spec((seq, seq)), _const_spec((seq, seq))],
        out_specs=blk,
        out_shape=jax.ShapeDtypeStruct((n_batch, seq, d_f), bf16),
        compiler_params=_cparams("arbitrary"),
        name="fourier_pos_ctx",
    )(pa.reshape(view), pb.reshape(view), cpos, mspos)


def _fourier_lat_kernel(pa_ref, pb_ref, c_ref, ms_ref, o_ref):
    y = jnp.dot(c_ref[...], pa_ref[0], preferred_element_type=f32)
    y = y + jnp.dot(ms_ref[...], pb_ref[0], preferred_element_type=f32)
    o_ref[0] = y.astype(bf16)


def _fourier_lat(pa, pb, c2, ms2, n_batch, seq, batch0):
    t, d_f = pa.shape
    view = (t // seq, seq, d_f)
    tq = min(FOURIER_TQ, seq)
    pblk = pl.BlockSpec((1, seq, d_f), lambda b, q: (batch0 + b, 0, 0))
    cblk = pl.BlockSpec((tq, seq), lambda b, q: (q, 0))
    return pl.pallas_call(
        _fourier_lat_kernel,
        grid=(n_batch, seq // tq),
        in_specs=[pblk, pblk, cblk, cblk],
        out_specs=pl.BlockSpec((1, tq, d_f), lambda b, q: (b, q, 0)),
        out_shape=jax.ShapeDtypeStruct((n_batch, seq, d_f), bf16),
        compiler_params=_cparams("arbitrary", "arbitrary"),
        name="fourier_pos_lat",
    )(pa.reshape(view), pb.reshape(view), c2, ms2)


def _gelu_tanh(x):
    return x * (0.5 * (1.0 + jnp.tanh(math.sqrt(2.0 / math.pi) * (x + 0.044715 * (x * x * x)))))


def _lru_kernel(*refs, direction, combine, nb, lc, ch, pitch, n_chunks):
    n_slab = ch // LANES
    (ux, hp, hn, cw, cb, wg, bg, lam, h0) = refs[:9]
    if combine:
        (hf, ug, y_out, st_out) = refs[9:13]
    else:
        (hf_out, st_out) = refs[9:11]
    x_scr = refs[-3 * n_slab - 1:-2 * n_slab - 1]
    a_scr = refs[-2 * n_slab - 1:-n_slab - 1]
    b_scr = refs[-n_slab - 1:-1]
    car = refs[-1]
    c = pl.program_id(2)
    cc = c if direction == 0 else n_chunks - 1 - c

    @pl.when(c == 0)
    def _():
        car[...] = h0[...]

    z = -lam[...]
    softplus = jnp.maximum(z, 0.0) + jnp.log1p(jnp.exp(-jnp.abs(z)))
    first_tap = SUBLANES - CONV_PAD_LEFT
    for b in range(nb):
        for s in range(n_slab):
            sl = slice(s * LANES, (s + 1) * LANES)
            x_scr[s][0:SUBLANES, :] = jnp.where(cc > 0, hp[b, :, sl], 0.0)
            x_scr[s][SUBLANES:SUBLANES + lc, :] = ux[b, :, sl]
            x_scr[s][SUBLANES + lc:2 * SUBLANES + lc, :] = jnp.where(cc < n_chunks - 1, hn[b, :, sl], 0.0)
        xc = cb[...]
        for k in range(CONV_W):
            tap = jnp.concatenate([x_scr[s][first_tap + k:first_tap + k + lc, :] for s in range(n_slab)], axis=1)
            xc = xc + tap * cw[k:k + 1, :]
        gates = jnp.dot(xc.astype(bf16), wg[0], preferred_element_type=f32) + bg[0]
        r = _sigmoid(gates[:, :ch])
        i = _sigmoid(gates[:, ch:])
        log_a = (-RG_LRU_C * r) * softplus
        a = jnp.exp(log_a)
        th = jnp.tanh(log_a)
        mult = jnp.sqrt((-2.0 * th) / (1.0 - th))
        bx = mult * (i * xc)
        for s in range(n_slab):
            a_scr[s][b * pitch:b * pitch + lc, :] = a[:, s * LANES:(s + 1) * LANES]
            b_scr[s][b * pitch:b * pitch + lc, :] = bx[:, s * LANES:(s + 1) * LANES]

    def steps(k, hs):
        base = k * SCAN_UNROLL if direction == 0 else lc - (k + 1) * SCAN_UNROLL
        base = pl.multiple_of(base, SCAN_UNROLL)
        hs = list(hs)
        for j in range(SCAN_UNROLL):
            t = base + (j if direction == 0 else SCAN_UNROLL - 1 - j)
            for s in range(n_slab):
                idx = pl.ds(t, nb, stride=pitch)
                hs[s] = a_scr[s][idx, :] * hs[s] + b_scr[s][idx, :]
                b_scr[s][idx, :] = hs[s]
        return tuple(hs)

    hs = lax.fori_loop(0, lc // SCAN_UNROLL, steps,
                       tuple(car[:, s * LANES:(s + 1) * LANES] for s in range(n_slab)))
    for s in range(n_slab):
        car[:, s * LANES:(s + 1) * LANES] = hs[s]
    st_out[...] = car[...]

    for b in range(nb):
        for s in range(n_slab):
            sl = slice(s * LANES, (s + 1) * LANES)
            h = b_scr[s][b * pitch:b * pitch + lc, :]
            if combine:
                y_out[b, :, sl] = ((hf[b, :, sl] + h) * _gelu_tanh(ug[b, :, sl])).astype(bf16)
            else:
                hf_out[b, :, sl] = h


def _lru_pass(direction, ux, ug, hf, conv_w, conv_b, wg, bg, lam, h0, n_batch, seq, batch0, nb):
    t, d_l = ux.shape
    ch = d_l // N_HEADS
    lc = min(LRU_CHUNK, seq)
    n_chunks = seq // lc
    pitch = lc + SUBLANES
    view = (t // seq, seq, d_l)
    assert batch0 % nb == 0 and n_batch % nb == 0
    g0 = batch0 // nb
    combine = direction == 1
    pos = (lambda c: c) if direction == 0 else (lambda c: n_chunks - 1 - c)
    l8 = lc // SUBLANES
    n8 = seq // SUBLANES
    main = pl.BlockSpec((nb, lc, ch), lambda g, h, c: (g0 + g, pos(c), h))
    prev = pl.BlockSpec((nb, SUBLANES, ch), lambda g, h, c: (g0 + g, jnp.maximum(pos(c) * l8 - 1, 0), h))
    nxt = pl.BlockSpec((nb, SUBLANES, ch), lambda g, h, c: (g0 + g, jnp.minimum((pos(c) + 1) * l8, n8 - 1), h))
    local = pl.BlockSpec((nb, lc, ch), lambda g, h, c: (g, pos(c), h))
    head_row = lambda rows: pl.BlockSpec((rows, ch), lambda g, h, c: (0, h))
    state = pl.BlockSpec((nb, ch), lambda g, h, c: (g, h))
    in_specs = [main, prev, nxt, head_row(CONV_W), head_row(1),
                pl.BlockSpec((1, ch, 2 * ch), lambda g, h, c: (h, 0, 0)),
                pl.BlockSpec((1, 1, 2 * ch), lambda g, h, c: (h, 0, 0)),
                head_row(1), state]
    ux3 = ux.reshape(view)
    args = [ux3, ux3, ux3, conv_w, conv_b, wg, bg, lam, h0]
    scratch = [pltpu.VMEM((lc + 2 * SUBLANES, LANES), f32) for _ in range(ch // LANES)]
    scratch += [pltpu.VMEM((nb * pitch, LANES), f32) for _ in range(2 * (ch // LANES))]
    scratch.append(pltpu.VMEM((nb, ch), f32))
    st_shape = jax.ShapeDtypeStruct((n_batch, d_l), f32)
    if combine:
        in_specs += [local, main]
        args += [hf, ug.reshape(view)]
    out_specs = [local, state]
    out_shape = [jax.ShapeDtypeStruct((n_batch, seq, d_l), bf16 if combine else f32), st_shape]
    kern = functools.partial(_lru_kernel, direction=direction, combine=combine,
                             nb=nb, lc=lc, ch=ch, pitch=pitch, n_chunks=n_chunks)
    return pl.pallas_call(
        kern,
        grid=(n_batch // nb, N_HEADS, n_chunks),
        in_specs=in_specs, out_specs=out_specs, out_shape=out_shape,
        scratch_shapes=scratch,
        compiler_params=_cparams("arbitrary", "arbitrary", "arbitrary"),
        name="rglru_dir%d" % direction,
    )(*args)


def _outproj_kernel(*refs, d_f, n_ctx_tiles, n_x, routed):
    (g1, sh2, sc2, gnf, gnl, n2g, wo) = refs[4 + n_x:11 + n_x]
    if routed:
        (rw, rb, x1_out, hm_out, lg_out) = refs[11 + n_x:]
    else:
        (x1_out, hm_out) = refs[11 + n_x:]
    yf = _stream_value(refs[0:2], n_ctx_tiles)
    yl = _stream_value(refs[2:4], n_ctx_tiles)
    x = _stream_value(refs[4:4 + n_x], n_ctx_tiles)
    yfn = _rms(yf.astype(f32), gnf[...]).astype(bf16)
    yln = _rms(yl.astype(f32), gnl[...]).astype(bf16)
    out = jnp.dot(yfn, wo[:d_f, :], preferred_element_type=f32)
    out = out + jnp.dot(yln, wo[d_f:, :], preferred_element_type=f32)
    x1 = x + g1[0] * out
    x1_out[...] = x1
    hm = _rms(x1, n2g[...]) * (1.0 + sc2[0]) + sh2[0]
    hm_out[...] = hm.astype(hm_out.dtype)
    if routed:
        ne = rw.shape[1] // 2
        hi = hm.astype(bf16)
        lo = (hm - hi.astype(f32)).astype(bf16)
        big = jnp.dot(hi, rw[...], preferred_element_type=f32)
        small = jnp.dot(lo, rw[:, :ne], preferred_element_type=f32)
        lg_out[...] = big[:, :ne] + (big[:, ne:] + small) + rb[...]


def _out_proj(yf, yl, x, t, mods_l, gnf, gnl, n2g, w_out, cond_of_tile, router):
    d = w_out.shape[1]
    d_f = yf[0].shape[1]
    d_l = yl[0].shape[1]
    nct = yf[0].shape[0] // TM
    routed = router is not None
    mod = lambda j: pl.BlockSpec((1, 1, d), lambda i: (cond_of_tile(i), 0, j))
    tok = lambda n: pl.BlockSpec((TM, n), lambda i: (i, 0))
    yf_specs, yf_args = _stream_specs(yf, d_f, nct)
    yl_specs, yl_args = _stream_specs(yl, d_l, nct)
    x_specs, x_args = _stream_specs(x, d, nct)
    in_specs = yf_specs + yl_specs + x_specs + [
        mod(2), mod(3), mod(4), _const_spec((1, d_f)), _const_spec((1, d_l)), _const_spec((1, d)),
        _const_spec((d_f + d_l, d))]
    args = yf_args + yl_args + x_args + [mods_l, mods_l, mods_l, gnf, gnl, n2g, w_out]
    out_specs = [tok(d), tok(d)]
    out_shape = [jax.ShapeDtypeStruct((t, d), f32), jax.ShapeDtypeStruct((t, d), f32 if routed else bf16)]
    if routed:
        rw, rb = router
        in_specs += [_const_spec(rw.shape), _const_spec(rb.shape)]
        args += [rw, rb]
        out_specs.append(tok(LANES))
        out_shape.append(jax.ShapeDtypeStruct((t, LANES), f32))
    return pl.pallas_call(
        functools.partial(_outproj_kernel, d_f=d_f, n_ctx_tiles=nct, n_x=len(x_args), routed=routed),
        grid=(t // TM,),
        in_specs=in_specs, out_specs=out_specs, out_shape=out_shape,
        compiler_params=_cparams("arbitrary"),
        name="out_proj",
    )(*args)


def _ffn_kernel(*refs, final, n_ctx_tiles):
    if final:
        hm, x1, g2, wg, wu, wd, fg = refs[:7]
        o_refs = refs[7:9]
    else:
        hm, x1, g2, wg, wu, wd = refs[:6]
        o_refs = refs[6:7]
    acc = refs[-1] if final else o_refs[0]
    f = pl.program_id(1)

    @pl.when(f == 0)
    def _():
        acc[...] = jnp.zeros(acc.shape, f32)

    xb = hm[...]
    g = jnp.dot(xb, wg[...], preferred_element_type=f32)
    u = jnp.dot(xb, wu[...], preferred_element_type=f32)
    h = ((g * _sigmoid(g)) * u).astype(bf16)
    acc[...] += jnp.dot(h, wd[...], preferred_element_type=f32)

    @pl.when(f == pl.num_programs(1) - 1)
    def _():
        y = x1[...] + g2[0] * acc[...]
        _stream_store(o_refs, _rms(y, fg[...]) if final else y, n_ctx_tiles)


def _final_out(t, t_ctx, d, rows, final):
    shapes = (jax.ShapeDtypeStruct((t_ctx, d), f32), jax.ShapeDtypeStruct((t - t_ctx, d), f32))
    specs, _ = _stream_specs(shapes if final else None, d, t_ctx // rows, rows)
    return specs, (list(shapes) if final else [jax.ShapeDtypeStruct((t, d), f32)])


def _ffn_dense(hm, x1, mods_l, wg, wu, wd, m, cond_of_tile, final_g, t_ctx):
    t, d = x1.shape
    dff = wg.shape[2]
    final = final_g is not None
    tmd = TMD_FINAL if final else TMD
    tok = pl.BlockSpec((tmd, d), lambda i, f: (i, 0))
    in_specs = [tok, tok, pl.BlockSpec((1, 1, d), lambda i, f: (cond_of_tile(tmd)(i), 0, 5)),
                pl.BlockSpec((None, d, TFD), lambda i, f: (m, 0, f)),
                pl.BlockSpec((None, d, TFD), lambda i, f: (m, 0, f)),
                pl.BlockSpec((None, TFD, d), lambda i, f: (m, f, 0))]
    args = [hm, x1, mods_l, wg, wu, wd]
    if final:
        in_specs.append(pl.BlockSpec((1, d), lambda i, f: (0, 0)))
        args.append(final_g)
    out_specs, out_shape = _final_out(t, t_ctx, d, tmd, final)
    out = pl.pallas_call(
        functools.partial(_ffn_kernel, final=final, n_ctx_tiles=t_ctx // tmd),
        grid=(t // tmd, dff // TFD),
        in_specs=in_specs, out_specs=out_specs, out_shape=out_shape,
        scratch_shapes=[pltpu.VMEM((tmd, d), f32)] if final else [],
        compiler_params=pltpu.CompilerParams(dimension_semantics=("arbitrary", "arbitrary"),
                                             vmem_limit_bytes=MOE_VMEM_LIMIT_BYTES),
        name="ffn_dense",
    )(*args)
    return tuple(out) if final else out[0]


def _route_kernel(lg_ref, meta_ref, cnt_ref, run, *, n_exp):
    i = pl.program_id(0)
    tr = lg_ref.shape[0]

    @pl.when(i == 0)
    def _():
        run[...] = jnp.zeros(run.shape, f32)

    lane = lax.broadcasted_iota(i32, (tr, LANES), 1)
    lane_f = lane.astype(f32)
    neg = jnp.float32(-jnp.inf)
    lg = jnp.where(lane < n_exp, lg_ref[...], neg)
    m1 = jnp.max(lg, axis=1, keepdims=True)
    i1 = jnp.min(jnp.where(lg == m1, lane_f, float(LANES)), axis=1, keepdims=True)
    oh1 = lane_f == i1
    lg2 = jnp.where(oh1, neg, lg)
    m2 = jnp.max(lg2, axis=1, keepdims=True)
    i2 = jnp.min(jnp.where(lg2 == m2, lane_f, float(LANES)), axis=1, keepdims=True)
    oh2 = lane_f == i2
    e = jnp.exp(m2 - m1)
    g1 = 1.0 / (1.0 + e)
    g2 = e / (1.0 + e)
    sel = jnp.where(oh1 | oh2, 1.0, 0.0)
    rr = lax.broadcasted_iota(i32, (tr, tr), 0)
    rc = lax.broadcasted_iota(i32, (tr, tr), 1)
    tri = jnp.where(rc < rr, 1.0, 0.0).astype(bf16)
    pos = jnp.dot(tri, sel.astype(bf16), preferred_element_type=f32) + run[...]
    p1 = jnp.sum(jnp.where(oh1, pos, 0.0), axis=1, keepdims=True)
    p2 = jnp.sum(jnp.where(oh2, pos, 0.0), axis=1, keepdims=True)
    run[...] = run[...] + jnp.sum(sel, axis=0, keepdims=True)
    cnt_ref[...] = run[...]
    cols = (i1, i2, p1, p2, g1, g2)
    meta = jnp.zeros((tr, LANES), f32)
    for k, v in enumerate(cols):
        meta = jnp.where(lane == k, v, meta)
    meta_ref[...] = meta


def _route(logits, n_exp):
    t = logits.shape[0]
    return pl.pallas_call(
        functools.partial(_route_kernel, n_exp=n_exp),
        grid=(t // TM,),
        in_specs=[pl.BlockSpec((TM, LANES), lambda i: (i, 0))],
        out_specs=[pl.BlockSpec((TM, LANES), lambda i: (i, 0)), pl.BlockSpec((1, LANES), lambda i: (0, 0))],
        out_shape=[jax.ShapeDtypeStruct((t, LANES), f32), jax.ShapeDtypeStruct((1, LANES), f32)],
        scratch_shapes=[pltpu.VMEM((1, LANES), f32)],
        compiler_params=_cparams("arbitrary"),
        name="route_top2",
    )(logits)


def _row_copy(src, src_row, dst, dst_row, sem):
    return pltpu.make_async_copy(src.at[pl.ds(src_row, 1), :], dst.at[pl.ds(dst_row, 1), :], sem)


def _scatter_kernel(d1, d2, zt, hm_ref, xs_out, zbuf, sem, zsem):
    i = pl.program_id(0)

    @pl.when(i == 0)
    def _():
        zbuf[...] = jnp.zeros(zbuf.shape, f32)
        n_fill = TMM // zbuf.shape[0]

        def fills(k, start):
            fresh = (k == 0) | (zt[k] != zt[jnp.maximum(k - 1, 0)])

            @pl.when(fresh)
            def _():
                for q in range(n_fill):
                    dst = xs_out.at[pl.ds(zt[k] * TMM + q * zbuf.shape[0], zbuf.shape[0]), :]
                    cp = pltpu.make_async_copy(zbuf, dst, zsem.at[0])
                    cp.start() if start else cp.wait()

        lax.fori_loop(0, zt.shape[0], lambda k, c: (fills(k, True), c)[1], 0)
        lax.fori_loop(0, zt.shape[0], lambda k, c: (fills(k, False), c)[1], 0)

    def issue(r, c):
        t = i * TROW + r
        _row_copy(hm_ref, r, xs_out, d1[t], sem.at[0]).start(priority=0)
        _row_copy(hm_ref, r, xs_out, d2[t], sem.at[1]).start(priority=1)
        return c

    def drain(r, c):
        _row_copy(hm_ref, 0, xs_out, 0, sem.at[0]).wait()
        _row_copy(hm_ref, 0, xs_out, 0, sem.at[1]).wait()
        return c

    lax.fori_loop(0, TROW, issue, 0, unroll=ROW_UNROLL)
    lax.fori_loop(0, TROW, drain, 0, unroll=ROW_UNROLL)


def _scatter_rows(d1, d2, zero_tiles, hm, n_rows):
    t, d = hm.shape
    return pl.pallas_call(
        _scatter_kernel,
        grid_spec=pltpu.PrefetchScalarGridSpec(
            num_scalar_prefetch=3, grid=(t // TROW,),
            in_specs=[pl.BlockSpec((TROW, d), lambda i, a, b, z: (i, 0))],
            out_specs=pl.BlockSpec(memory_space=pl.ANY),
            scratch_shapes=[pltpu.VMEM((TROW, d), f32), pltpu.SemaphoreType.DMA((2,)),
                            pltpu.SemaphoreType.DMA((1,))]),
        out_shape=jax.ShapeDtypeStruct((n_rows, d), f32),
        compiler_params=_cparams("arbitrary"),
        name="moe_scatter_rows",
    )(d1, d2, zero_tiles, hm)


def _moe_kernel(te, nu, nv, xs, wg, wu, wd, o_ref, xb):
    del te, nu
    j = pl.program_id(0)
    f = pl.program_id(1)
    rows = nv[j]

    @pl.when(f == 0)
    def _():
        o_ref[...] = jnp.zeros(o_ref.shape, f32)

    @pl.when((f == 0) & (rows > 0))
    def _():
        xb[...] = xs[...].astype(bf16)

    def swiglu_rows(n_rows):
        x = xb[:n_rows, :]
        g = jnp.dot(x, wg[0].astype(bf16), preferred_element_type=f32)
        u = jnp.dot(x, wu[0].astype(bf16), preferred_element_type=f32)
        h = ((g * _sigmoid(g)) * u).astype(bf16)
        o_ref[:n_rows, :] += jnp.dot(h, wd[0].astype(bf16), preferred_element_type=f32)

    n_piece = xb.shape[0] // MOE_SUB
    for p in range(1, n_piece + 1):
        @pl.when((rows > (p - 1) * MOE_SUB) & (rows <= p * MOE_SUB))
        def _(p=p):
            swiglu_rows(p * MOE_SUB)


def _moe_grouped(tile_expert, n_used, tile_rows, xs, wg, wu, wd, m):
    r, d = xs.shape
    dff = wg.shape[3]
    nf = dff // TFM
    row = lambda j, f, te, nu, nv: (jnp.minimum(j, nu[0] - 1), 0)
    fidx = lambda j, f, nu: jnp.where(j < nu[0], f, nf - 1)
    return pl.pallas_call(
        _moe_kernel,
        grid_spec=pltpu.PrefetchScalarGridSpec(
            num_scalar_prefetch=3, grid=(r // TMM, nf),
            in_specs=[pl.BlockSpec((TMM, d), row, pipeline_mode=pl.Buffered(MOE_XS_BUFFERS)),
                      pl.BlockSpec((None, 1, d, TFM), lambda j, f, te, nu, nv: (m, te[j], 0, fidx(j, f, nu))),
                      pl.BlockSpec((None, 1, d, TFM), lambda j, f, te, nu, nv: (m, te[j], 0, fidx(j, f, nu))),
                      pl.BlockSpec((None, 1, TFM, d), lambda j, f, te, nu, nv: (m, te[j], fidx(j, f, nu), 0))],
            out_specs=pl.BlockSpec((TMM, d), lambda j, f, te, nu, nv: (j, 0)),
            scratch_shapes=[pltpu.VMEM((TMM, d), bf16)]),
        out_shape=jax.ShapeDtypeStruct((r, d), f32),
        compiler_params=pltpu.CompilerParams(dimension_semantics=("arbitrary", "arbitrary"),
                                             vmem_limit_bytes=MOE_VMEM_LIMIT_BYTES),
        name="moe_grouped_swiglu",
    )(tile_expert, n_used, tile_rows, xs, wg, wu, wd)


def _combine_kernel(*refs, final, n_ctx_tiles):
    if final:
        d1, d2, x1, g2, meta, fg, ys = refs[:7]
        o_refs = refs[7:9]
    else:
        d1, d2, x1, g2, meta, ys = refs[:6]
        o_refs = refs[6:7]
    buf, sem = refs[-2:]
    i = pl.program_id(0)
    slot = i % 2

    def issue(step, s):
        def body(r, c):
            t = step * TROW + r
            _row_copy(ys, d1[t], buf.at[s, 0], r, sem.at[s, 0]).start(priority=0)
            _row_copy(ys, d2[t], buf.at[s, 1], r, sem.at[s, 1]).start(priority=1)
            return c
        lax.fori_loop(0, TROW, body, 0, unroll=ROW_UNROLL)

    def drain(r, c):
        _row_copy(ys, 0, buf.at[slot, 0], 0, sem.at[slot, 0]).wait()
        _row_copy(ys, 0, buf.at[slot, 1], 0, sem.at[slot, 1]).wait()
        return c

    @pl.when(i == 0)
    def _():
        issue(0, 0)

    @pl.when(i + 1 < pl.num_programs(0))
    def _():
        issue(i + 1, 1 - slot)

    lax.fori_loop(0, TROW, drain, 0, unroll=ROW_UNROLL)
    m = meta[...]
    ff = m[:, 4:5] * buf[slot, 0] + m[:, 5:6] * buf[slot, 1]
    y = x1[...] + g2[0] * ff
    _stream_store(o_refs, _rms(y, fg[...]) if final else y, n_ctx_tiles)


def _combine_rows(d1, d2, x1, mods_l, meta, ys, cond_of_tile_row, final_g, t_ctx):
    t, d = x1.shape
    final = final_g is not None
    tok = lambda n: pl.BlockSpec((TROW, n), lambda i, a, b: (i, 0))
    in_specs = [tok(d), pl.BlockSpec((1, 1, d), lambda i, a, b: (cond_of_tile_row(i), 0, 5)), tok(LANES)]
    args = [x1, mods_l, meta]
    if final:
        in_specs.append(pl.BlockSpec((1, d), lambda i, a, b: (0, 0)))
        args.append(final_g)
    in_specs.append(pl.BlockSpec(memory_space=pl.ANY))
    args.append(ys)
    out_specs, out_shape = _final_out(t, t_ctx, d, TROW, final)
    out = pl.pallas_call(
        functools.partial(_combine_kernel, final=final, n_ctx_tiles=t_ctx // TROW),
        grid_spec=pltpu.PrefetchScalarGridSpec(
            num_scalar_prefetch=2, grid=(t // TROW,),
            in_specs=in_specs, out_specs=out_specs,
            scratch_shapes=[pltpu.VMEM((2, TOP_K, TROW, d), f32), pltpu.SemaphoreType.DMA((2, TOP_K))]),
        out_shape=out_shape,
        compiler_params=_cparams("arbitrary"),
        name="moe_combine_rows",
    )(d1, d2, *args)
    return tuple(out) if final else out[0]


def _moe_ffn(hm, logits, x1, mods_l, wg, wu, wd, m, cond_of_tile_row, final_g, t_ctx):
    t, d = hm.shape
    n_exp = wg.shape[1]
    assert (TOP_K * t) % TMM == 0 and TMM % MOE_SUB == 0
    meta, counts = _route(logits, n_exp)
    cnt = counts[0, :n_exp].astype(i32)
    padded = ((cnt + TMM - 1) // TMM) * TMM
    ends = jnp.cumsum(padded)
    off = ends - padded
    e1 = meta[:, 0].astype(i32)
    e2 = meta[:, 1].astype(i32)
    d1 = off[e1] + meta[:, 2].astype(i32)
    d2 = off[e2] + meta[:, 3].astype(i32)
    n_tiles = (TOP_K * t) // TMM + n_exp
    tile_start = jnp.arange(n_tiles, dtype=i32) * TMM
    n_used = (ends[-1] // TMM).reshape(1).astype(i32)
    tile_expert = jnp.minimum(jnp.sum((tile_start[:, None] >= ends[None, :]).astype(i32), axis=1), n_exp - 1)
    tile_rows = jnp.clip((off + cnt)[tile_expert] - tile_start, 0, TMM)
    tile_rows = jnp.where(tile_start < ends[-1], tile_rows, 0)
    tile_expert = jnp.where(tile_start < ends[-1], tile_expert, tile_expert[n_used[0] - 1])
    last_tile = jnp.maximum(ends - 1, 0) // TMM
    spare_tile = jnp.minimum(n_used[0] + jnp.arange(n_exp, dtype=i32), n_tiles - 1)
    zero_tiles = jnp.concatenate([last_tile, jnp.maximum(spare_tile, last_tile[-1])]).astype(i32)
    xs = _scatter_rows(d1, d2, zero_tiles, hm, n_tiles * TMM)
    ys = _moe_grouped(tile_expert, n_used, tile_rows, xs, wg, wu, wd, m)
    return _combine_rows(d1, d2, x1, mods_l, meta, ys, cond_of_tile_row, final_g, t_ctx)


@functools.lru_cache(maxsize=None)
def _dft_tables(n_ctx, rows, grid_w, dh):
    def cos_sin(k, period):
        ang = 2.0 * np.pi * (k % period).astype(np.float64) / period
        return np.cos(ang), np.sin(ang)

    k = np.arange(dh)
    cc, sc = cos_sin(k[:, None] * k[None, :], dh)
    chan = np.stack([cc, sc]).astype(np.float32)
    t = np.arange(n_ctx)
    cp, sp = cos_sin(t[:, None] * t[None, :], n_ctx)
    s_ctx = 1.0 / math.sqrt(n_ctx * dh)
    n = np.arange(rows * grid_w)
    r, c = n // grid_w, n % grid_w
    period = rows * grid_w // math.gcd(rows, grid_w)
    phase = (r[:, None] * r[None, :]) * (period // rows) + (c[:, None] * c[None, :]) * (period // grid_w)
    c2, s2 = cos_sin(phase, period)
    s_lat = 1.0 / math.sqrt(rows * grid_w * dh)
    as_f32 = lambda a: np.asarray(a, np.float32)
    return chan, as_f32(cp * s_ctx), as_f32(-sp * s_ctx), as_f32(c2 * s_lat), as_f32(-s2 * s_lat)


def kernel(x_prompt, x_sample, state_lru, c, c_ctx, norm1_g, norm2_g, w_mod, b_mod, w_in, w_fourier,
           gn_fourier_g, conv_w, conv_b, w_r, b_r, w_i, b_i, lam, gn_lru_g, w_out, ffn_w_gate, ffn_w_up,
           ffn_w_down, router_w, router_b, moe_w_gate, moe_w_up, moe_w_down, final_g):
    nb_ctx, seq_ctx, d = x_prompt.shape
    nb_lat, seq_lat, _ = x_sample.shape
    depth = w_mod.shape[0]
    d_f = w_fourier.shape[1] * w_fourier.shape[2]
    d_l = lam.shape[2]
    dh = d_f // N_HEADS
    t_ctx = nb_ctx * seq_ctx
    t_lat = nb_lat * seq_lat
    assert t_ctx % TMD == 0 and seq_lat % TMD == 0 and TMD % TM == 0
    assert t_ctx % seq_lat == 0 and seq_lat % GRID_W == 0
    assert d_l // N_HEADS == dh and dh % LANES == 0 and TM % TROW == 0

    def cond_of(tile_rows):
        return lambda i: jnp.where(i * tile_rows < t_ctx, 0,
                                   1 + jnp.maximum(i * tile_rows - t_ctx, 0) // seq_lat)

    cond_tm, cond_row = cond_of(TM), cond_of(TROW)

    n_cond = 1 + nb_lat
    ncp = -(-n_cond // SUBLANES) * SUBLANES
    cond = jnp.concatenate([c_ctx[None, :], c, jnp.zeros((ncp - n_cond, d), f32)], axis=0)
    mods = _mods(cond.T, w_mod, b_mod, n_cond).reshape(depth, ncp, 1, N_MOD * d)

    chan, cp, msp, c2, ms2 = _dft_tables(seq_ctx, seq_lat // GRID_W, GRID_W, dh)
    chan = jnp.asarray(chan)
    cp, msp, c2, ms2 = (jnp.asarray(a).astype(bf16) for a in (cp, msp, c2, ms2))

    t = t_ctx + t_lat
    x = (x_prompt.reshape(t_ctx, d), x_sample.reshape(t_lat, d))
    zeros_h0 = jnp.zeros((nb_ctx, d_l), f32)
    ctx_states = []
    for l in range(depth):
        mods_l = mods[l]
        ab = _fold_channel_dft(chan, w_fourier[l])
        pa, pb, ux, ug = _in_proj(x, t, t_ctx // TM, mods_l, norm1_g[l][None, :], w_in[l].astype(bf16), ab,
                                  cond_tm, d_f, d_l)

        yf_c = _fourier_ctx(pa, pb, cp, msp, nb_ctx, seq_ctx).reshape(t_ctx, d_f)
        yf_l = _fourier_lat(pa, pb, c2, ms2, nb_lat, seq_lat, t_ctx // seq_lat).reshape(t_lat, d_f)

        def gate_w(dr):
            wgt = jnp.concatenate([w_r[l, dr], w_i[l, dr]], axis=-1).astype(bf16)
            bias = jnp.concatenate([b_r[l, dr].reshape(N_HEADS, 1, dh), b_i[l, dr].reshape(N_HEADS, 1, dh)], axis=-1)
            return wgt, bias

        yls = []
        states = []
        for (n_batch, seq, batch0, nb, h0) in (
                (nb_ctx, seq_ctx, 0, SUBLANES, (zeros_h0, zeros_h0)),
                (nb_lat, seq_lat, t_ctx // seq_lat, nb_lat,
                 (state_lru[:, l, 0], state_lru[:, l, 1]))):
            wg0, bg0 = gate_w(0)
            hf, st_f = _lru_pass(0, ux, None, None, conv_w[l], conv_b[l][None, :], wg0, bg0,
                                 lam[l, 0][None, :], h0[0], n_batch, seq, batch0, nb)
            wg1, bg1 = gate_w(1)
            yl, st_b = _lru_pass(1, ux, ug, hf, conv_w[l], conv_b[l][None, :], wg1, bg1,
                                 lam[l, 1][None, :], h0[1], n_batch, seq, batch0, nb)
            yls.append(yl.reshape(n_batch * seq, d_l))
            states.append((st_f, st_b))
        ctx_states.append(jnp.stack(states[0], axis=1))
        mixed = ((yf_c, yf_l), (yls[0], yls[1]))

        m = l // 2
        last = l == depth - 1
        fg = final_g[None, :] if last else None
        if l % 2 == 0:
            x1, hm = _out_proj(*mixed, x, t, mods_l, gn_fourier_g[l][None, :], gn_lru_g[l][None, :],
                               norm2_g[l][None, :], w_out[l].astype(bf16), cond_tm, None)
            x = _ffn_dense(hm, x1, mods_l, ffn_w_gate.astype(bf16), ffn_w_up.astype(bf16),
                           ffn_w_down.astype(bf16), m, cond_of, fg, t_ctx)
        else:
            n_exp = router_w.shape[2]
            rw = jnp.zeros((d, LANES), f32).at[:, :n_exp].set(router_w[m])
            rw_hi = rw.astype(bf16)
            rw_lo = (rw - rw_hi.astype(f32)).astype(bf16)
            rb = jnp.zeros((1, LANES), f32).at[0, :n_exp].set(router_b[m])
            x1, hm, logits = _out_proj(*mixed, x, t, mods_l, gn_fourier_g[l][None, :], gn_lru_g[l][None, :],
                                       norm2_g[l][None, :], w_out[l].astype(bf16), cond_tm,
                                       (jnp.concatenate([rw_hi, rw_lo], axis=1), rb))
            x = _moe_ffn(hm, logits, x1, mods_l, moe_w_gate, moe_w_up, moe_w_down, m, cond_row, fg, t_ctx)
    y_prompt = x[0].reshape(nb_ctx, seq_ctx, d)
    y_sample = x[1].reshape(nb_lat, seq_lat, d)
    new_state = jnp.stack(ctx_states, axis=1).astype(state_lru.dtype)
    return (y_prompt, y_sample, new_state)
```

```python
import functools
import math

import jax
import jax.numpy as jnp
import numpy as np
from jax import lax
from jax.experimental import pallas as pl
from jax.experimental.pallas import tpu as pltpu

f32 = jnp.float32
bf16 = jnp.bfloat16
i32 = jnp.int32

GRID_W = 64
N_HEADS = 4
CONV_W = 4
CONV_PAD_LEFT = 2
RG_LRU_C = 8.0
N_MOD = 6
TOP_K = 2
EPS = 1e-6

LANES = 128
SUBLANES = 8
VMEM_LIMIT_BYTES = 56 * 1024 * 1024

MOE_VMEM_LIMIT_BYTES = 60 * 1024 * 1024

TM = 512
TMD = 1024
TMD_FINAL = 512
TFD = 512
TMM = 1024
MOE_SUB = 512
TFM = 256
LRU_CHUNK = 256
SCAN_UNROLL = 8
TROW = 256
ROW_UNROLL = 8
FOURIER_TQ = 512


def _cparams(*sem):
    return pltpu.CompilerParams(dimension_semantics=sem, vmem_limit_bytes=VMEM_LIMIT_BYTES)


def _rms(x, g):
    return x * lax.rsqrt(jnp.mean(x * x, axis=-1, keepdims=True) + EPS) * g


def _sigmoid(x):
    return 1.0 / (1.0 + jnp.exp(-x))


def _const_spec(shape):
    nd = len(shape)
    return pl.BlockSpec(shape, lambda *_: (0,) * nd, pipeline_mode=pl.Buffered(1))


def _mod_kernel(ct_ref, w_ref, b_ref, o_ref, sb, *, n_cond, kc):
    d = ct_ref.shape[0]
    n_lt = w_ref.shape[2] // LANES

    @pl.when((pl.program_id(0) == 0) & (pl.program_id(1) == 0))
    def _():
        c = ct_ref[...]
        s = c * _sigmoid(c)
        for r in range(n_cond):
            sb[r] = jnp.broadcast_to(s[:, r:r + 1], (d, LANES))

    def body(k, accs):
        k0 = pl.multiple_of(k * kc, kc)
        ws = [w_ref[0, pl.ds(k0, kc), lt * LANES:(lt + 1) * LANES] for lt in range(n_lt)]
        out = []
        for r in range(n_cond):
            sr = sb[r, pl.ds(k0, kc), :]
            for lt in range(n_lt):
                part = (ws[lt] * sr).reshape(kc // SUBLANES, SUBLANES, LANES).sum(axis=0)
                out.append(accs[r * n_lt + lt] + part)
        return tuple(out)

    zero = jnp.zeros((SUBLANES, LANES), f32)
    accs = lax.fori_loop(0, d // kc, body, tuple(zero for _ in range(n_cond * n_lt)), unroll=2)
    o_ref[...] = jnp.zeros(o_ref.shape, f32)
    for r in range(n_cond):
        for lt in range(n_lt):
            sl = slice(lt * LANES, (lt + 1) * LANES)
            o_ref[0, r:r + 1, sl] = jnp.sum(accs[r * n_lt + lt], axis=0, keepdims=True) + b_ref[0, :, sl]


def _mods(cond_t, w_mod, b_mod, n_cond):
    depth, d, n6 = w_mod.shape
    ncp = cond_t.shape[1]
    tn = 768
    return pl.pallas_call(
        functools.partial(_mod_kernel, n_cond=n_cond, kc=16),
        grid=(depth, n6 // tn),
        in_specs=[pl.BlockSpec((d, ncp), lambda l, j: (0, 0)),
                  pl.BlockSpec((1, d, tn), lambda l, j: (l, 0, j)),
                  pl.BlockSpec((1, 1, tn), lambda l, j: (l, 0, j))],
        out_specs=pl.BlockSpec((1, ncp, tn), lambda l, j: (l, 0, j)),
        out_shape=jax.ShapeDtypeStruct((depth, ncp, n6), f32),
        scratch_shapes=[pltpu.VMEM((n_cond, d, LANES), f32)],
        compiler_params=_cparams("arbitrary", "arbitrary"),
        name="adaln_mods",
    )(cond_t, w_mod, b_mod.reshape(depth, 1, n6))


def _ab_kernel(cs_ref, wf_ref, o_ref):
    dh = wf_ref.shape[1]
    w = wf_ref[0]
    o_ref[0, :, :dh] = jnp.dot(cs_ref[0], w, precision=lax.Precision.HIGHEST,
                               preferred_element_type=f32).astype(bf16)
    o_ref[0, :, dh:] = jnp.dot(cs_ref[1], w, precision=lax.Precision.HIGHEST,
                               preferred_element_type=f32).astype(bf16)


def _fold_channel_dft(cs, w_f):
    nh, dh, _ = w_f.shape
    return pl.pallas_call(
        _ab_kernel,
        grid=(nh,),
        in_specs=[pl.BlockSpec((2, dh, dh), lambda h: (0, 0, 0)),
                  pl.BlockSpec((1, dh, dh), lambda h: (h, 0, 0))],
        out_specs=pl.BlockSpec((1, dh, 2 * dh), lambda h: (h, 0, 0)),
        out_shape=jax.ShapeDtypeStruct((nh, dh, 2 * dh), bf16),
        compiler_params=_cparams("arbitrary"),
        name="fold_channel_dft",
    )(cs, w_f)


def _stream_specs(x, n_cols, n_ctx_tiles, rows=TM):
    if not isinstance(x, tuple):
        return [pl.BlockSpec((rows, n_cols), lambda i, *_: (i, 0))], [x]
    ctx = pl.BlockSpec((rows, n_cols), lambda i, *_: (jnp.minimum(i, n_ctx_tiles - 1), 0))
    lat = pl.BlockSpec((rows, n_cols), lambda i, *_: (jnp.maximum(i - n_ctx_tiles, 0), 0))
    return [ctx, lat], list(x)


def _stream_value(refs, n_ctx_tiles):
    if len(refs) == 1:
        return refs[0][...]
    return jnp.where(pl.program_id(0) < n_ctx_tiles, refs[0][...], refs[1][...])


def _stream_store(refs, y, n_ctx_tiles):
    if len(refs) == 1:
        refs[0][...] = y
        return
    is_ctx = pl.program_id(0) < n_ctx_tiles

    @pl.when(is_ctx)
    def _():
        refs[0][...] = y

    @pl.when(jnp.logical_not(is_ctx))
    def _():
        refs[1][...] = y


def _inproj_kernel(*refs, d_f, d_l, n_x, n_ctx_tiles):
    x = _stream_value(refs[:n_x], n_ctx_tiles)
    (sh_ref, sc_ref, g_ref, w_ref, ab_ref, pa_ref, pb_ref, ux_ref, ug_ref) = refs[n_x:]
    h = _rms(x, g_ref[...]) * (1.0 + sc_ref[0]) + sh_ref[0]
    u = jnp.dot(h.astype(bf16), w_ref[...], preferred_element_type=f32)
    nh = ab_ref.shape[0]
    dh = d_f // nh
    for hh in range(nh):
        sl = slice(hh * dh, (hh + 1) * dh)
        p = jnp.dot(u[:, sl].astype(bf16), ab_ref[hh], preferred_element_type=f32)
        pa_ref[:, sl] = p[:, :dh].astype(bf16)
        pb_ref[:, sl] = p[:, dh:].astype(bf16)
    ux_ref[...] = u[:, d_f:d_f + d_l]
    ug_ref[...] = u[:, d_f + d_l:]


def _in_proj(x, t, n_ctx_tiles, mods_l, g, w_in, ab, cond_of_tile, d_f, d_l):
    d, d_in = w_in.shape
    nh, dh, _ = ab.shape
    mod = lambda j: pl.BlockSpec((1, 1, d), lambda i: (cond_of_tile(i), 0, j))
    tok = lambda n: pl.BlockSpec((TM, n), lambda i: (i, 0))
    x_specs, x_args = _stream_specs(x, d, n_ctx_tiles)
    return pl.pallas_call(
        functools.partial(_inproj_kernel, d_f=d_f, d_l=d_l, n_x=len(x_args), n_ctx_tiles=n_ctx_tiles),
        grid=(t // TM,),
        in_specs=x_specs + [mod(0), mod(1), _const_spec((1, d)), _const_spec((d, d_in)),
                            _const_spec((nh, dh, 2 * dh))],
        out_specs=[tok(d_f), tok(d_f), tok(d_l), tok(d_l)],
        out_shape=[jax.ShapeDtypeStruct((t, d_f), bf16), jax.ShapeDtypeStruct((t, d_f), bf16),
                   jax.ShapeDtypeStruct((t, d_l), f32), jax.ShapeDtypeStruct((t, d_l), f32)],
        compiler_params=_cparams("arbitrary"),
        name="in_proj",
    )(*x_args, mods_l, mods_l, g, w_in, ab)


def _fourier_ctx_kernel(pa_ref, pb_ref, c_ref, ms_ref, o_ref):
    for b in range(pa_ref.shape[0]):
        y = jnp.dot(c_ref[...], pa_ref[b], preferred_element_type=f32)
        y = y + jnp.dot(ms_ref[...], pb_ref[b], preferred_element_type=f32)
        o_ref[b] = y.astype(bf16)


def _fourier_ctx(pa, pb, cpos, mspos, n_batch, seq):
    t, d_f = pa.shape
    nb = 4
    view = (t // seq, seq, d_f)
    blk = pl.BlockSpec((nb, seq, d_f), lambda g: (g, 0, 0))
    return pl.pallas_call(
        _fourier_ctx_kernel,
        grid=(n_batch // nb,),
        in_specs=[blk, blk, _const_spec((seq, seq)), _const_spec((seq, seq))],
        out_specs=blk,
        out_shape=jax.ShapeDtypeStruct((n_batch, seq, d_f), bf16),
        compiler_params=_cparams("arbitrary"),
        name="fourier_pos_ctx",
    )(pa.reshape(view), pb.reshape(view), cpos, mspos)


def _fourier_lat_kernel(pa_ref, pb_ref, c_ref, ms_ref, o_ref):
    y = jnp.dot(c_ref[...], pa_ref[0], preferred_element_type=f32)
    y = y + jnp.dot(ms_ref[...], pb_ref[0], preferred_element_type=f32)
    o_ref[0] = y.astype(bf16)


def _fourier_lat(pa, pb, c2, ms2, n_batch, seq, batch0):
    t, d_f = pa.shape
    view = (t // seq, seq, d_f)
    tq = min(FOURIER_TQ, seq)
    pblk = pl.BlockSpec((1, seq, d_f), lambda b, q: (batch0 + b, 0, 0))
    cblk = pl.BlockSpec((tq, seq), lambda b, q: (q, 0))
    return pl.pallas_call(
        _fourier_lat_kernel,
        grid=(n_batch, seq // tq),
        in_specs=[pblk, pblk, cblk, cblk],
        out_specs=pl.BlockSpec((1, tq, d_f), lambda b, q: (b, q, 0)),
        out_shape=jax.ShapeDtypeStruct((n_batch, seq, d_f), bf16),
        compiler_params=_cparams("arbitrary", "arbitrary"),
        name="fourier_pos_lat",
    )(pa.reshape(view), pb.reshape(view), c2, ms2)


def _gelu_tanh(x):
    c = math.sqrt(2.0 / math.pi)
    half = 0.5 * x
    return half + half * jnp.tanh(x * (c + (c * 0.044715) * (x * x)))


def _lru_kernel(*refs, direction, combine, nb, lc, ch, pitch, n_chunks):
    n_slab = ch // LANES
    (ux, hp, hn, cw, cb, wg, bg, lam, h0) = refs[:9]
    if combine:
        (hf, ug, y_out, st_out) = refs[9:13]
    else:
        (hf_out, st_out) = refs[9:11]
    x_scr = refs[-3 * n_slab - 1:-2 * n_slab - 1]
    a_scr = refs[-2 * n_slab - 1:-n_slab - 1]
    b_scr = refs[-n_slab - 1:-1]
    car = refs[-1]
    c = pl.program_id(2)
    cc = c if direction == 0 else n_chunks - 1 - c

    @pl.when(c == 0)
    def _():
        car[...] = h0[...]

    z = -lam[...]
    softplus = jnp.maximum(z, 0.0) + jnp.log1p(jnp.exp(-jnp.abs(z)))
    first_tap = SUBLANES - CONV_PAD_LEFT
    for b in range(nb):
        for s in range(n_slab):
            sl = slice(s * LANES, (s + 1) * LANES)
            x_scr[s][0:SUBLANES, :] = jnp.where(cc > 0, hp[b, :, sl], 0.0)
            x_scr[s][SUBLANES:SUBLANES + lc, :] = ux[b, :, sl]
            x_scr[s][SUBLANES + lc:2 * SUBLANES + lc, :] = jnp.where(cc < n_chunks - 1, hn[b, :, sl], 0.0)
        xc = cb[...]
        for k in range(CONV_W):
            tap = jnp.concatenate([x_scr[s][first_tap + k:first_tap + k + lc, :] for s in range(n_slab)], axis=1)
            xc = xc + tap * cw[k:k + 1, :]
        gates = jnp.dot(xc.astype(bf16), wg[0], preferred_element_type=f32) + bg[0]
        r = _sigmoid(gates[:, :ch])
        i = _sigmoid(gates[:, ch:])
        log_a = (-RG_LRU_C * r) * softplus
        a = jnp.exp(log_a)
        th = jnp.tanh(log_a)
        mult = jnp.sqrt((-2.0 * th) / (1.0 - th))
        bx = mult * (i * xc)
        for s in range(n_slab):
            a_scr[s][b * pitch:b * pitch + lc, :] = a[:, s * LANES:(s + 1) * LANES]
            b_scr[s][b * pitch:b * pitch + lc, :] = bx[:, s * LANES:(s + 1) * LANES]

    def steps(k, hs):
        base = k * SCAN_UNROLL if direction == 0 else lc - (k + 1) * SCAN_UNROLL
        base = pl.multiple_of(base, SCAN_UNROLL)
        hs = list(hs)
        for j in range(SCAN_UNROLL):
            t = base + (j if direction == 0 else SCAN_UNROLL - 1 - j)
            for s in range(n_slab):
                idx = pl.ds(t, nb, stride=pitch)
                hs[s] = a_scr[s][idx, :] * hs[s] + b_scr[s][idx, :]
                b_scr[s][idx, :] = hs[s]
        return tuple(hs)

    hs = lax.fori_loop(0, lc // SCAN_UNROLL, steps,
                       tuple(car[:, s * LANES:(s + 1) * LANES] for s in range(n_slab)))
    for s in range(n_slab):
        car[:, s * LANES:(s + 1) * LANES] = hs[s]
    st_out[...] = car[...]

    for b in range(nb):
        for s in range(n_slab):
            sl = slice(s * LANES, (s + 1) * LANES)
            h = b_scr[s][b * pitch:b * pitch + lc, :]
            if combine:
                y_out[b, :, sl] = ((hf[b, :, sl] + h) * _gelu_tanh(ug[b, :, sl])).astype(bf16)
            else:
                hf_out[b, :, sl] = h


def _lru_pass(direction, ux, ug, hf, conv_w, conv_b, wg, bg, lam, h0, n_batch, seq, batch0, nb):
    t, d_l = ux.shape
    ch = d_l // N_HEADS
    lc = min(LRU_CHUNK, seq)
    n_chunks = seq // lc
    pitch = lc + SUBLANES
    view = (t // seq, seq, d_l)
    assert batch0 % nb == 0 and n_batch % nb == 0
    g0 = batch0 // nb
    combine = direction == 1
    pos = (lambda c: c) if direction == 0 else (lambda c: n_chunks - 1 - c)
    l8 = lc // SUBLANES
    n8 = seq // SUBLANES
    main = pl.BlockSpec((nb, lc, ch), lambda g, h, c: (g0 + g, pos(c), h))
    prev = pl.BlockSpec((nb, SUBLANES, ch), lambda g, h, c: (g0 + g, jnp.maximum(pos(c) * l8 - 1, 0), h))
    nxt = pl.BlockSpec((nb, SUBLANES, ch), lambda g, h, c: (g0 + g, jnp.minimum((pos(c) + 1) * l8, n8 - 1), h))
    local = pl.BlockSpec((nb, lc, ch), lambda g, h, c: (g, pos(c), h))
    head_row = lambda rows: pl.BlockSpec((rows, ch), lambda g, h, c: (0, h))
    state = pl.BlockSpec((nb, ch), lambda g, h, c: (g, h))
    in_specs = [main, prev, nxt, head_row(CONV_W), head_row(1),
                pl.BlockSpec((1, ch, 2 * ch), lambda g, h, c: (h, 0, 0)),
                pl.BlockSpec((1, 1, 2 * ch), lambda g, h, c: (h, 0, 0)),
                head_row(1), state]
    ux3 = ux.reshape(view)
    args = [ux3, ux3, ux3, conv_w, conv_b, wg, bg, lam, h0]
    scratch = [pltpu.VMEM((lc + 2 * SUBLANES, LANES), f32) for _ in range(ch // LANES)]
    scratch += [pltpu.VMEM((nb * pitch, LANES), f32) for _ in range(2 * (ch // LANES))]
    scratch.append(pltpu.VMEM((nb, ch), f32))
    st_shape = jax.ShapeDtypeStruct((n_batch, d_l), f32)
    if combine:
        in_specs += [local, main]
        args += [hf, ug.reshape(view)]
    out_specs = [local, state]
    out_shape = [jax.ShapeDtypeStruct((n_batch, seq, d_l), bf16 if combine else f32), st_shape]
    kern = functools.partial(_lru_kernel, direction=direction, combine=combine,
                             nb=nb, lc=lc, ch=ch, pitch=pitch, n_chunks=n_chunks)
    return pl.pallas_call(
        kern,
        grid=(n_batch // nb, N_HEADS, n_chunks),
        in_specs=in_specs, out_specs=out_specs, out_shape=out_shape,
        scratch_shapes=scratch,
        compiler_params=_cparams("arbitrary", "arbitrary", "arbitrary"),
        name="rglru_dir%d" % direction,
    )(*args)


def _outproj_kernel(*refs, d_f, n_ctx_tiles, n_x, routed):
    (g1, sh2, sc2, gnf, gnl, n2g, wo) = refs[4 + n_x:11 + n_x]
    if routed:
        (rw, rb, x1_out, hm_out, lg_out) = refs[11 + n_x:]
    else:
        (x1_out, hm_out) = refs[11 + n_x:]
    yf = _stream_value(refs[0:2], n_ctx_tiles)
    yl = _stream_value(refs[2:4], n_ctx_tiles)
    x = _stream_value(refs[4:4 + n_x], n_ctx_tiles)
    yfn = _rms(yf.astype(f32), gnf[...]).astype(bf16)
    yln = _rms(yl.astype(f32), gnl[...]).astype(bf16)
    out = jnp.dot(yfn, wo[:d_f, :], preferred_element_type=f32)
    out = out + jnp.dot(yln, wo[d_f:, :], preferred_element_type=f32)
    x1 = x + g1[0] * out
    x1_out[...] = x1
    hm = _rms(x1, n2g[...]) * (1.0 + sc2[0]) + sh2[0]
    hm_out[...] = hm.astype(hm_out.dtype)
    if routed:
        ne = rw.shape[1] // 2
        hi = hm.astype(bf16)
        lo = (hm - hi.astype(f32)).astype(bf16)
        big = jnp.dot(hi, rw[...], preferred_element_type=f32)
        small = jnp.dot(lo, rw[:, :ne], preferred_element_type=f32)
        lg_out[...] = big[:, :ne] + (big[:, ne:] + small) + rb[...]


def _out_proj(yf, yl, x, t, mods_l, gnf, gnl, n2g, w_out, cond_of_tile, router):
    d = w_out.shape[1]
    d_f = yf[0].shape[1]
    d_l = yl[0].shape[1]
    nct = yf[0].shape[0] // TM
    routed = router is not None
    mod = lambda j: pl.BlockSpec((1, 1, d), lambda i: (cond_of_tile(i), 0, j))
    tok = lambda n: pl.BlockSpec((TM, n), lambda i: (i, 0))
    yf_specs, yf_args = _stream_specs(yf, d_f, nct)
    yl_specs, yl_args = _stream_specs(yl, d_l, nct)
    x_specs, x_args = _stream_specs(x, d, nct)
    in_specs = yf_specs + yl_specs + x_specs + [
        mod(2), mod(3), mod(4), _const_spec((1, d_f)), _const_spec((1, d_l)), _const_spec((1, d)),
        _const_spec((d_f + d_l, d))]
    args = yf_args + yl_args + x_args + [mods_l, mods_l, mods_l, gnf, gnl, n2g, w_out]
    out_specs = [tok(d), tok(d)]
    out_shape = [jax.ShapeDtypeStruct((t, d), f32), jax.ShapeDtypeStruct((t, d), f32 if routed else bf16)]
    if routed:
        rw, rb = router
        in_specs += [_const_spec(rw.shape), _const_spec(rb.shape)]
        args += [rw, rb]
        out_specs.append(tok(LANES))
        out_shape.append(jax.ShapeDtypeStruct((t, LANES), f32))
    return pl.pallas_call(
        functools.partial(_outproj_kernel, d_f=d_f, n_ctx_tiles=nct, n_x=len(x_args), routed=routed),
        grid=(t // TM,),
        in_specs=in_specs, out_specs=out_specs, out_shape=out_shape,
        compiler_params=_cparams("arbitrary"),
        name="out_proj",
    )(*args)


def _ffn_kernel(*refs, final, n_ctx_tiles):
    if final:
        hm, x1, g2, wg, wu, wd, fg = refs[:7]
        o_refs = refs[7:9]
    else:
        hm, x1, g2, wg, wu, wd = refs[:6]
        o_refs = refs[6:7]
    acc = refs[-1] if final else o_refs[0]
    f = pl.program_id(1)

    @pl.when(f == 0)
    def _():
        acc[...] = jnp.zeros(acc.shape, f32)

    xb = hm[...]
    g = jnp.dot(xb, wg[...], preferred_element_type=f32)
    u = jnp.dot(xb, wu[...], preferred_element_type=f32)
    h = ((g * _sigmoid(g)) * u).astype(bf16)
    acc[...] += jnp.dot(h, wd[...], preferred_element_type=f32)

    @pl.when(f == pl.num_programs(1) - 1)
    def _():
        y = x1[...] + g2[0] * acc[...]
        _stream_store(o_refs, _rms(y, fg[...]) if final else y, n_ctx_tiles)


def _final_out(t, t_ctx, d, rows, final):
    shapes = (jax.ShapeDtypeStruct((t_ctx, d), f32), jax.ShapeDtypeStruct((t - t_ctx, d), f32))
    specs, _ = _stream_specs(shapes if final else None, d, t_ctx // rows, rows)
    return specs, (list(shapes) if final else [jax.ShapeDtypeStruct((t, d), f32)])


def _ffn_dense(hm, x1, mods_l, wg, wu, wd, m, cond_of_tile, final_g, t_ctx):
    t, d = x1.shape
    dff = wg.shape[2]
    final = final_g is not None
    tmd = TMD_FINAL if final else TMD
    tok = pl.BlockSpec((tmd, d), lambda i, f: (i, 0))
    in_specs = [tok, tok, pl.BlockSpec((1, 1, d), lambda i, f: (cond_of_tile(tmd)(i), 0, 5)),
                pl.BlockSpec((None, d, TFD), lambda i, f: (m, 0, f)),
                pl.BlockSpec((None, d, TFD), lambda i, f: (m, 0, f)),
                pl.BlockSpec((None, TFD, d), lambda i, f: (m, f, 0))]
    args = [hm, x1, mods_l, wg, wu, wd]
    if final:
        in_specs.append(pl.BlockSpec((1, d), lambda i, f: (0, 0)))
        args.append(final_g)
    out_specs, out_shape = _final_out(t, t_ctx, d, tmd, final)
    out = pl.pallas_call(
        functools.partial(_ffn_kernel, final=final, n_ctx_tiles=t_ctx // tmd),
        grid=(t // tmd, dff // TFD),
        in_specs=in_specs, out_specs=out_specs, out_shape=out_shape,
        scratch_shapes=[pltpu.VMEM((tmd, d), f32)] if final else [],
        compiler_params=pltpu.CompilerParams(dimension_semantics=("arbitrary", "arbitrary"),
                                             vmem_limit_bytes=MOE_VMEM_LIMIT_BYTES),
        name="ffn_dense",
    )(*args)
    return tuple(out) if final else out[0]


def _route_kernel(lg_ref, meta_ref, cnt_ref, run, *, n_exp):
    i = pl.program_id(0)
    tr = lg_ref.shape[0]

    @pl.when(i == 0)
    def _():
        run[...] = jnp.zeros(run.shape, f32)

    lane = lax.broadcasted_iota(i32, (tr, LANES), 1)
    lane_f = lane.astype(f32)
    neg = jnp.float32(-jnp.inf)
    lg = jnp.where(lane < n_exp, lg_ref[...], neg)
    m1 = jnp.max(lg, axis=1, keepdims=True)
    i1 = jnp.min(jnp.where(lg == m1, lane_f, float(LANES)), axis=1, keepdims=True)
    oh1 = lane_f == i1
    lg2 = jnp.where(oh1, neg, lg)
    m2 = jnp.max(lg2, axis=1, keepdims=True)
    i2 = jnp.min(jnp.where(lg2 == m2, lane_f, float(LANES)), axis=1, keepdims=True)
    oh2 = lane_f == i2
    e = jnp.exp(m2 - m1)
    g1 = 1.0 / (1.0 + e)
    g2 = e / (1.0 + e)
    sel = jnp.where(oh1 | oh2, 1.0, 0.0)
    rr = lax.broadcasted_iota(i32, (tr, tr), 0)
    rc = lax.broadcasted_iota(i32, (tr, tr), 1)
    tri = jnp.where(rc < rr, 1.0, 0.0).astype(bf16)
    pos = jnp.dot(tri, sel.astype(bf16), preferred_element_type=f32) + run[...]
    p1 = jnp.sum(jnp.where(oh1, pos, 0.0), axis=1, keepdims=True)
    p2 = jnp.sum(jnp.where(oh2, pos, 0.0), axis=1, keepdims=True)
    run[...] = run[...] + jnp.sum(sel, axis=0, keepdims=True)
    cnt_ref[...] = run[...]
    cols = (i1, i2, p1, p2, g1, g2)
    meta = jnp.zeros((tr, LANES), f32)
    for k, v in enumerate(cols):
        meta = jnp.where(lane == k, v, meta)
    meta_ref[...] = meta


def _route(logits, n_exp):
    t = logits.shape[0]
    return pl.pallas_call(
        functools.partial(_route_kernel, n_exp=n_exp),
        grid=(t // TM,),
        in_specs=[pl.BlockSpec((TM, LANES), lambda i: (i, 0))],
        out_specs=[pl.BlockSpec((TM, LANES), lambda i: (i, 0)), pl.BlockSpec((1, LANES), lambda i: (0, 0))],
        out_shape=[jax.ShapeDtypeStruct((t, LANES), f32), jax.ShapeDtypeStruct((1, LANES), f32)],
        scratch_shapes=[pltpu.VMEM((1, LANES), f32)],
        compiler_params=_cparams("arbitrary"),
        name="route_top2",
    )(logits)


def _row_copy(src, src_row, dst, dst_row, sem):
    return pltpu.make_async_copy(src.at[pl.ds(src_row, 1), :], dst.at[pl.ds(dst_row, 1), :], sem)


def _scatter_kernel(d1, d2, zt, hm_ref, xs_out, zbuf, sem, zsem):
    i = pl.program_id(0)

    @pl.when(i == 0)
    def _():
        zbuf[...] = jnp.zeros(zbuf.shape, f32)
        n_fill = TMM // zbuf.shape[0]

        def fills(k, start):
            fresh = (k == 0) | (zt[k] != zt[jnp.maximum(k - 1, 0)])

            @pl.when(fresh)
            def _():
                for q in range(n_fill):
                    dst = xs_out.at[pl.ds(zt[k] * TMM + q * zbuf.shape[0], zbuf.shape[0]), :]
                    cp = pltpu.make_async_copy(zbuf, dst, zsem.at[0])
                    cp.start() if start else cp.wait()

        lax.fori_loop(0, zt.shape[0], lambda k, c: (fills(k, True), c)[1], 0)
        lax.fori_loop(0, zt.shape[0], lambda k, c: (fills(k, False), c)[1], 0)

    def issue(r, c):
        t = i * TROW + r
        _row_copy(hm_ref, r, xs_out, d1[t], sem.at[0]).start(priority=0)
        _row_copy(hm_ref, r, xs_out, d2[t], sem.at[1]).start(priority=1)
        return c

    def drain(r, c):
        _row_copy(hm_ref, 0, xs_out, 0, sem.at[0]).wait()
        _row_copy(hm_ref, 0, xs_out, 0, sem.at[1]).wait()
        return c

    lax.fori_loop(0, TROW, issue, 0, unroll=ROW_UNROLL)
    lax.fori_loop(0, TROW, drain, 0, unroll=ROW_UNROLL)


def _scatter_rows(d1, d2, zero_tiles, hm, n_rows):
    t, d = hm.shape
    return pl.pallas_call(
        _scatter_kernel,
        grid_spec=pltpu.PrefetchScalarGridSpec(
            num_scalar_prefetch=3, grid=(t // TROW,),
            in_specs=[pl.BlockSpec((TROW, d), lambda i, a, b, z: (i, 0))],
            out_specs=pl.BlockSpec(memory_space=pl.ANY),
            scratch_shapes=[pltpu.VMEM((TROW, d), f32), pltpu.SemaphoreType.DMA((2,)),
                            pltpu.SemaphoreType.DMA((1,))]),
        out_shape=jax.ShapeDtypeStruct((n_rows, d), f32),
        compiler_params=_cparams("arbitrary"),
        name="moe_scatter_rows",
    )(d1, d2, zero_tiles, hm)


def _moe_kernel(te, nu, nv, xs, wg, wu, wd, o_ref, xb):
    del te, nu
    j = pl.program_id(0)
    f = pl.program_id(1)
    rows = nv[j]

    @pl.when(f == 0)
    def _():
        o_ref[...] = jnp.zeros(o_ref.shape, f32)

    @pl.when((f == 0) & (rows > 0))
    def _():
        xb[...] = xs[...].astype(bf16)

    def swiglu_rows(n_rows):
        x = xb[:n_rows, :]
        g = jnp.dot(x, wg[0].astype(bf16), preferred_element_type=f32)
        u = jnp.dot(x, wu[0].astype(bf16), preferred_element_type=f32)
        h = ((g * _sigmoid(g)) * u).astype(bf16)
        o_ref[:n_rows, :] += jnp.dot(h, wd[0].astype(bf16), preferred_element_type=f32)

    n_piece = xb.shape[0] // MOE_SUB
    for p in range(1, n_piece + 1):
        @pl.when((rows > (p - 1) * MOE_SUB) & (rows <= p * MOE_SUB))
        def _(p=p):
            swiglu_rows(p * MOE_SUB)


def _moe_grouped(tile_expert, n_used, tile_rows, xs, wg, wu, wd, m):
    r, d = xs.shape
    dff = wg.shape[3]
    nf = dff // TFM
    row = lambda j, f, te, nu, nv: (jnp.minimum(j, nu[0] - 1), 0)
    fidx = lambda j, f, nu: jnp.where(j < nu[0], f, nf - 1)
    return pl.pallas_call(
        _moe_kernel,
        grid_spec=pltpu.PrefetchScalarGridSpec(
            num_scalar_prefetch=3, grid=(r // TMM, nf),
            in_specs=[pl.BlockSpec((TMM, d), row),
                      pl.BlockSpec((None, 1, d, TFM), lambda j, f, te, nu, nv: (m, te[j], 0, fidx(j, f, nu))),
                      pl.BlockSpec((None, 1, d, TFM), lambda j, f, te, nu, nv: (m, te[j], 0, fidx(j, f, nu))),
                      pl.BlockSpec((None, 1, TFM, d), lambda j, f, te, nu, nv: (m, te[j], fidx(j, f, nu), 0))],
            out_specs=pl.BlockSpec((TMM, d), lambda j, f, te, nu, nv: (j, 0)),
            scratch_shapes=[pltpu.VMEM((TMM, d), bf16)]),
        out_shape=jax.ShapeDtypeStruct((r, d), f32),
        compiler_params=pltpu.CompilerParams(dimension_semantics=("arbitrary", "arbitrary"),
                                             vmem_limit_bytes=MOE_VMEM_LIMIT_BYTES),
        name="moe_grouped_swiglu",
    )(tile_expert, n_used, tile_rows, xs, wg, wu, wd)


def _combine_kernel(*refs, final, n_ctx_tiles):
    if final:
        d1, d2, x1, g2, meta, fg, ys = refs[:7]
        o_refs = refs[7:9]
    else:
        d1, d2, x1, g2, meta, ys = refs[:6]
        o_refs = refs[6:7]
    buf, sem = refs[-2:]
    i = pl.program_id(0)
    slot = i % 2

    def issue(step, s):
        def body(r, c):
            t = step * TROW + r
            _row_copy(ys, d1[t], buf.at[s, 0], r, sem.at[s, 0]).start(priority=0)
            _row_copy(ys, d2[t], buf.at[s, 1], r, sem.at[s, 1]).start(priority=1)
            return c
        lax.fori_loop(0, TROW, body, 0, unroll=ROW_UNROLL)

    def drain(r, c):
        _row_copy(ys, 0, buf.at[slot, 0], 0, sem.at[slot, 0]).wait()
        _row_copy(ys, 0, buf.at[slot, 1], 0, sem.at[slot, 1]).wait()
        return c

    @pl.when(i == 0)
    def _():
        issue(0, 0)

    @pl.when(i + 1 < pl.num_programs(0))
    def _():
        issue(i + 1, 1 - slot)

    lax.fori_loop(0, TROW, drain, 0, unroll=ROW_UNROLL)
    m = meta[...]
    ff = m[:, 4:5] * buf[slot, 0] + m[:, 5:6] * buf[slot, 1]
    y = x1[...] + g2[0] * ff
    _stream_store(o_refs, _rms(y, fg[...]) if final else y, n_ctx_tiles)


def _combine_rows(d1, d2, x1, mods_l, meta, ys, cond_of_tile_row, final_g, t_ctx):
    t, d = x1.shape
    final = final_g is not None
    tok = lambda n: pl.BlockSpec((TROW, n), lambda i, a, b: (i, 0))
    in_specs = [tok(d), pl.BlockSpec((1, 1, d), lambda i, a, b: (cond_of_tile_row(i), 0, 5)), tok(LANES)]
    args = [x1, mods_l, meta]
    if final:
        in_specs.append(pl.BlockSpec((1, d), lambda i, a, b: (0, 0)))
        args.append(final_g)
    in_specs.append(pl.BlockSpec(memory_space=pl.ANY))
    args.append(ys)
    out_specs, out_shape = _final_out(t, t_ctx, d, TROW, final)
    out = pl.pallas_call(
        functools.partial(_combine_kernel, final=final, n_ctx_tiles=t_ctx // TROW),
        grid_spec=pltpu.PrefetchScalarGridSpec(
            num_scalar_prefetch=2, grid=(t // TROW,),
            in_specs=in_specs, out_specs=out_specs,
            scratch_shapes=[pltpu.VMEM((2, TOP_K, TROW, d), f32), pltpu.SemaphoreType.DMA((2, TOP_K))]),
        out_shape=out_shape,
        compiler_params=_cparams("arbitrary"),
        name="moe_combine_rows",
    )(d1, d2, *args)
    return tuple(out) if final else out[0]


def _moe_ffn(hm, logits, x1, mods_l, wg, wu, wd, m, cond_of_tile_row, final_g, t_ctx):
    t, d = hm.shape
    n_exp = wg.shape[1]
    assert (TOP_K * t) % TMM == 0 and TMM % MOE_SUB == 0
    meta, counts = _route(logits, n_exp)
    cnt = counts[0, :n_exp].astype(i32)
    padded = ((cnt + TMM - 1) // TMM) * TMM
    ends = jnp.cumsum(padded)
    off = ends - padded
    e1 = meta[:, 0].astype(i32)
    e2 = meta[:, 1].astype(i32)
    d1 = off[e1] + meta[:, 2].astype(i32)
    d2 = off[e2] + meta[:, 3].astype(i32)
    n_tiles = (TOP_K * t) // TMM + n_exp
    tile_start = jnp.arange(n_tiles, dtype=i32) * TMM
    n_used = (ends[-1] // TMM).reshape(1).astype(i32)
    tile_expert = jnp.minimum(jnp.sum((tile_start[:, None] >= ends[None, :]).astype(i32), axis=1), n_exp - 1)
    tile_rows = jnp.clip((off + cnt)[tile_expert] - tile_start, 0, TMM)
    tile_rows = jnp.where(tile_start < ends[-1], tile_rows, 0)
    tile_expert = jnp.where(tile_start < ends[-1], tile_expert, tile_expert[n_used[0] - 1])
    last_tile = jnp.maximum(ends - 1, 0) // TMM
    spare_tile = jnp.minimum(n_used[0] + jnp.arange(n_exp, dtype=i32), n_tiles - 1)
    zero_tiles = jnp.concatenate([last_tile, jnp.maximum(spare_tile, last_tile[-1])]).astype(i32)
    xs = _scatter_rows(d1, d2, zero_tiles, hm, n_tiles * TMM)
    ys = _moe_grouped(tile_expert, n_used, tile_rows, xs, wg, wu, wd, m)
    return _combine_rows(d1, d2, x1, mods_l, meta, ys, cond_of_tile_row, final_g, t_ctx)


@functools.lru_cache(maxsize=None)
def _dft_tables(n_ctx, rows, grid_w, dh):
    def cos_sin(k, period):
        ang = 2.0 * np.pi * (k % period).astype(np.float64) / period
        return np.cos(ang), np.sin(ang)

    k = np.arange(dh)
    cc, sc = cos_sin(k[:, None] * k[None, :], dh)
    chan = np.stack([cc, sc]).astype(np.float32)
    t = np.arange(n_ctx)
    cp, sp = cos_sin(t[:, None] * t[None, :], n_ctx)
    s_ctx = 1.0 / math.sqrt(n_ctx * dh)
    n = np.arange(rows * grid_w)
    r, c = n // grid_w, n % grid_w
    period = rows * grid_w // math.gcd(rows, grid_w)
    phase = (r[:, None] * r[None, :]) * (period // rows) + (c[:, None] * c[None, :]) * (period // grid_w)
    c2, s2 = cos_sin(phase, period)
    s_lat = 1.0 / math.sqrt(rows * grid_w * dh)
    as_f32 = lambda a: np.asarray(a, np.float32)
    return chan, as_f32(cp * s_ctx), as_f32(-sp * s_ctx), as_f32(c2 * s_lat), as_f32(-s2 * s_lat)


def kernel(x_prompt, x_sample, state_lru, c, c_ctx, norm1_g, norm2_g, w_mod, b_mod, w_in, w_fourier,
           gn_fourier_g, conv_w, conv_b, w_r, b_r, w_i, b_i, lam, gn_lru_g, w_out, ffn_w_gate, ffn_w_up,
           ffn_w_down, router_w, router_b, moe_w_gate, moe_w_up, moe_w_down, final_g):
    nb_ctx, seq_ctx, d = x_prompt.shape
    nb_lat, seq_lat, _ = x_sample.shape
    depth = w_mod.shape[0]
    d_f = w_fourier.shape[1] * w_fourier.shape[2]
    d_l = lam.shape[2]
    dh = d_f // N_HEADS
    t_ctx = nb_ctx * seq_ctx
    t_lat = nb_lat * seq_lat
    assert t_ctx % TMD == 0 and seq_lat % TMD == 0 and TMD % TM == 0
    assert t_ctx % seq_lat == 0 and seq_lat % GRID_W == 0
    assert d_l // N_HEADS == dh and dh % LANES == 0 and TM % TROW == 0

    def cond_of(tile_rows):
        return lambda i: jnp.where(i * tile_rows < t_ctx, 0,
                                   1 + jnp.maximum(i * tile_rows - t_ctx, 0) // seq_lat)

    cond_tm, cond_row = cond_of(TM), cond_of(TROW)

    n_cond = 1 + nb_lat
    ncp = -(-n_cond // SUBLANES) * SUBLANES
    cond = jnp.concatenate([c_ctx[None, :], c, jnp.zeros((ncp - n_cond, d), f32)], axis=0)
    mods = _mods(cond.T, w_mod, b_mod, n_cond).reshape(depth, ncp, 1, N_MOD * d)

    chan, cp, msp, c2, ms2 = _dft_tables(seq_ctx, seq_lat // GRID_W, GRID_W, dh)
    chan = jnp.asarray(chan)
    cp, msp, c2, ms2 = (jnp.asarray(a).astype(bf16) for a in (cp, msp, c2, ms2))

    t = t_ctx + t_lat
    x = (x_prompt.reshape(t_ctx, d), x_sample.reshape(t_lat, d))
    zeros_h0 = jnp.zeros((nb_ctx, d_l), f32)
    ctx_states = []
    for l in range(depth):
        mods_l = mods[l]
        ab = _fold_channel_dft(chan, w_fourier[l])
        pa, pb, ux, ug = _in_proj(x, t, t_ctx // TM, mods_l, norm1_g[l][None, :], w_in[l].astype(bf16), ab,
                                  cond_tm, d_f, d_l)

        yf_c = _fourier_ctx(pa, pb, cp, msp, nb_ctx, seq_ctx).reshape(t_ctx, d_f)
        yf_l = _fourier_lat(pa, pb, c2, ms2, nb_lat, seq_lat, t_ctx // seq_lat).reshape(t_lat, d_f)

        def gate_w(dr):
            wgt = jnp.concatenate([w_r[l, dr], w_i[l, dr]], axis=-1).astype(bf16)
            bias = jnp.concatenate([b_r[l, dr].reshape(N_HEADS, 1, dh), b_i[l, dr].reshape(N_HEADS, 1, dh)], axis=-1)
            return wgt, bias

        yls = []
        states = []
        for (n_batch, seq, batch0, nb, h0) in (
                (nb_ctx, seq_ctx, 0, SUBLANES, (zeros_h0, zeros_h0)),
                (nb_lat, seq_lat, t_ctx // seq_lat, nb_lat,
                 (state_lru[:, l, 0], state_lru[:, l, 1]))):
            wg0, bg0 = gate_w(0)
            hf, st_f = _lru_pass(0, ux, None, None, conv_w[l], conv_b[l][None, :], wg0, bg0,
                                 lam[l, 0][None, :], h0[0], n_batch, seq, batch0, nb)
            wg1, bg1 = gate_w(1)
            yl, st_b = _lru_pass(1, ux, ug, hf, conv_w[l], conv_b[l][None, :], wg1, bg1,
                                 lam[l, 1][None, :], h0[1], n_batch, seq, batch0, nb)
            yls.append(yl.reshape(n_batch * seq, d_l))
            states.append((st_f, st_b))
        ctx_states.append(jnp.stack(states[0], axis=1))
        mixed = ((yf_c, yf_l), (yls[0], yls[1]))

        m = l // 2
        last = l == depth - 1
        fg = final_g[None, :] if last else None
        if l % 2 == 0:
            x1, hm = _out_proj(*mixed, x, t, mods_l, gn_fourier_g[l][None, :], gn_lru_g[l][None, :],
                               norm2_g[l][None, :], w_out[l].astype(bf16), cond_tm, None)
            x = _ffn_dense(hm, x1, mods_l, ffn_w_gate.astype(bf16), ffn_w_up.astype(bf16),
                           ffn_w_down.astype(bf16), m, cond_of, fg, t_ctx)
        else:
            n_exp = router_w.shape[2]
            rw = jnp.zeros((d, LANES), f32).at[:, :n_exp].set(router_w[m])
            rw_hi = rw.astype(bf16)
            rw_lo = (rw - rw_hi.astype(f32)).astype(bf16)
            rb = jnp.zeros((1, LANES), f32).at[0, :n_exp].set(router_b[m])
            x1, hm, logits = _out_proj(*mixed, x, t, mods_l, gn_fourier_g[l][None, :], gn_lru_g[l][None, :],
                                       norm2_g[l][None, :], w_out[l].astype(bf16), cond_tm,
                                       (jnp.concatenate([rw_hi, rw_lo], axis=1), rb))
            x = _moe_ffn(hm, logits, x1, mods_l, moe_w_gate, moe_w_up, moe_w_down, m, cond_row, fg, t_ctx)
    y_prompt = x[0].reshape(nb_ctx, seq_ctx, d)
    y_sample = x[1].reshape(nb_lat, seq_lat, d)
    new_state = jnp.stack(ctx_states, axis=1).astype(state_lru.dtype)
    return (y_prompt, y_sample, new_state)
```
